```python
import jax, jax.numpy as jnp
from jax import lax
import numpy as np

D_MODEL = 1024
BATCH = 2
SEQ = 16384
DEPTH = 1
DEC_BATCH = 128
DEC_SEQ = 8
PAST_LEN = 8192
PAGE_SIZE = 128

HEAD_DIM = 64
SB_HEADS = 8
DSA_HEADS = 8
DSA_KV_HEADS = 4
IDX_HEADS = 8
IDX_DIM = 32
TOPK_MAX = 256
ROPE_THETA = 500000.0
ROPE_FRACTION = 4
D_FF = -(-8 * D_MODEL // (3 * 256)) * 256
Q_BLOCK = 128
RMS_EPS = 1e-6
SB_W = SB_HEADS * HEAD_DIM
DSA_Q_W = DSA_HEADS * HEAD_DIM
DSA_KV_W = DSA_KV_HEADS * HEAD_DIM
IDX_Q_W = IDX_HEADS * IDX_DIM
SPLITS = (SB_W, SB_W, SB_W, DSA_Q_W, DSA_KV_W, DSA_KV_W, IDX_Q_W, IDX_DIM, IDX_HEADS, D_MODEL, D_MODEL)
IN_W = sum(SPLITS)

kernel_name = "stickbreak_dsa_gated_hybrid_step"


def rmsnorm(x, g):
    xf = x.astype(jnp.float32)
    y = xf * lax.rsqrt(jnp.mean(xf * xf, axis=-1, keepdims=True) + RMS_EPS)
    return (y * g.astype(jnp.float32)).astype(x.dtype)


def rope_partial(x, pos):
    d = x.shape[-1]
    half = d // ROPE_FRACTION // 2
    inv_freq = ROPE_THETA ** (-jnp.arange(half, dtype=jnp.float32) / half)
    ang = pos.astype(jnp.float32)[:, None] * inv_freq[None, :]
    cos = jnp.cos(ang)[:, None, :]
    sin = jnp.sin(ang)[:, None, :]
    xf = x.astype(jnp.float32)
    x1, x2, rest = xf[..., :half], xf[..., half:2 * half], xf[..., 2 * half:]
    out = jnp.concatenate([x1 * cos - x2 * sin, x1 * sin + x2 * cos, rest], axis=-1)
    return out.astype(x.dtype)


def layer_inputs(x, pos, g_mix, w_in):
    B, T, _ = x.shape
    h = rmsnorm(x, g_mix)
    u = jnp.einsum('btd,de->bte', h, w_in)
    points = [int(p) for p in np.cumsum(SPLITS)[:-1]]
    q_a, k_a, v_a, q_b, k_b, v_b, q_i, k_i, w_i, g_a, g_b = jnp.split(u, points, axis=-1)
    q_a = q_a.reshape(B, T, SB_HEADS, HEAD_DIM)
    k_a = k_a.reshape(B, T, SB_HEADS, HEAD_DIM)
    v_a = v_a.reshape(B, T, SB_HEADS, HEAD_DIM)
    q_b = rope_partial(q_b.reshape(B, T, DSA_HEADS, HEAD_DIM), pos)
    k_b = rope_partial(k_b.reshape(B, T, DSA_KV_HEADS, HEAD_DIM), pos)
    v_b = v_b.reshape(B, T, DSA_KV_HEADS, HEAD_DIM)
    q_i = rope_partial(q_i.reshape(B, T, IDX_HEADS, IDX_DIM), pos)
    k_i = rope_partial(k_i[:, :, None, :], pos)[:, :, 0, :]
    w_i = w_i * (IDX_HEADS ** -0.5)
    return q_a, k_a, v_a, q_b, k_b, v_b, q_i, k_i, w_i, g_a, g_b


def layer_outputs(x, o_a, o_b, g_a, g_b, w_pa, w_pb, w_o, g_ffn, w_gate, w_up, w_down):
    o_a = o_a.astype(x.dtype)
    o_b = o_b.astype(x.dtype)
    m = jax.nn.sigmoid(g_a) * (o_a @ w_pa) + jax.nn.sigmoid(g_b) * (o_b @ w_pb)
    x = x + (m @ w_o).astype(x.dtype)
    h = rmsnorm(x, g_ffn)
    f = (jax.nn.silu(h @ w_gate) * (h @ w_up)) @ w_down
    return x + f.astype(x.dtype)


def sb_prompt(q, k, v):
    B, S, H, D = q.shape
    scale = D ** -0.5
    kf = k.astype(jnp.float32)
    vf = v.astype(jnp.float32)
    key_pos = jnp.arange(S)

    def block(i):
        qb = lax.dynamic_slice_in_dim(q, i * Q_BLOCK, Q_BLOCK, axis=1).astype(jnp.float32)
        qpos = i * Q_BLOCK + jnp.arange(Q_BLOCK)
        z = jnp.einsum('bqhd,bshd->bhqs', qb, kf) * scale
        mask = key_pos[None, :] < qpos[:, None]
        l = jnp.where(mask, -jax.nn.softplus(z), 0.0)
        suffix = lax.cumsum(l, axis=3, reverse=True)
        a = jnp.exp(jnp.where(mask, z + suffix, -jnp.inf))
        return jnp.einsum('bhqs,bshd->bqhd', a, vf)

    out = lax.map(block, jnp.arange(S // Q_BLOCK))
    return out.transpose(1, 0, 2, 3, 4).reshape(B, S, H * D)


def sb_sample(q, k_new, v_new, cache_k, cache_v, page_table, layer):
    Bd, T, H, D = q.shape
    scale = D ** -0.5
    qf = q.astype(jnp.float32)
    z = jnp.einsum('bqhd,bshd->bhqs', qf, k_new.astype(jnp.float32)) * scale
    mask = jnp.arange(T)[None, :] < jnp.arange(T)[:, None]
    l = jnp.where(mask, -jax.nn.softplus(z), 0.0)
    suffix = lax.cumsum(l, axis=3, reverse=True)
    a = jnp.exp(jnp.where(mask, z + suffix, -jnp.inf))
    out0 = jnp.einsum('bhqs,bshd->bqhd', a, v_new.astype(jnp.float32))
    acc0 = jnp.sum(l, axis=-1)

    def step(carry, pages):
        acc, out = carry
        kp = cache_k[layer, pages].astype(jnp.float32)
        vp = cache_v[layer, pages].astype(jnp.float32)
        zp = jnp.einsum('bqhd,bshd->bhqs', qf, kp) * scale
        lp = -jax.nn.softplus(zp)
        sp = lax.cumsum(lp, axis=3, reverse=True) + acc[..., None]
        out = out + jnp.einsum('bhqs,bshd->bqhd', jnp.exp(zp + sp), vp)
        return (acc + jnp.sum(lp, axis=-1), out), None

    (_, out), _ = lax.scan(step, (acc0, out0), page_table.T, reverse=True)
    return out.reshape(Bd, T, H * D)


def gqa_over_selected(q, k_sel, v_sel, valid):
    B, T, H, D = q.shape
    kvh = k_sel.shape[3]
    qg = q.astype(jnp.float32).reshape(B, T, kvh, H // kvh, D)
    s = jnp.einsum('btgrd,btngd->btgrn', qg, k_sel.astype(jnp.float32)) * (D ** -0.5)
    s = jnp.where(valid[:, :, None, None, :], s, -jnp.inf)
    p = jax.nn.softmax(s, axis=-1)
    o = jnp.einsum('btgrn,btngd->btgrd', p, v_sel.astype(jnp.float32))
    return o.reshape(B, T, H * D)


def indexer_scores(q_idx, w_idx, k_idx):
    rel = jax.nn.relu(jnp.einsum('bqhd,bsd->bqhs', q_idx.astype(jnp.float32),
                                 k_idx.astype(jnp.float32)) * (IDX_DIM ** -0.5))
    return jnp.einsum('bqhs,bqh->bqs', rel, w_idx.astype(jnp.float32))


def dsa_prompt(q, k, v, q_idx, k_idx, w_idx):
    B, S, H, D = q.shape
    topk = min(TOPK_MAX, S // 4)
    key_pos = jnp.arange(S)
    bidx = jnp.arange(B)[:, None, None]

    def block(i):
        qpos = i * Q_BLOCK + jnp.arange(Q_BLOCK)
        qi = lax.dynamic_slice_in_dim(q_idx, i * Q_BLOCK, Q_BLOCK, axis=1)
        wi = lax.dynamic_slice_in_dim(w_idx, i * Q_BLOCK, Q_BLOCK, axis=1)
        score = indexer_scores(qi, wi, k_idx)
        score = jnp.where(key_pos[None, None, :] <= qpos[None, :, None], score, -jnp.inf)
        _, sel = lax.top_k(score, topk)
        valid = sel <= qpos[None, :, None]
        qb = lax.dynamic_slice_in_dim(q, i * Q_BLOCK, Q_BLOCK, axis=1)
        return gqa_over_selected(qb, k[bidx, sel], v[bidx, sel], valid)

    out = lax.map(block, jnp.arange(S // Q_BLOCK))
    return out.transpose(1, 0, 2, 3).reshape(B, S, H * D)


def dsa_sample(q, k_new, v_new, q_idx, k_idx_new, w_idx, cache_k, cache_v, cache_idx, page_table, layer):
    Bd, T, H, D = q.shape
    pg = cache_k.shape[2]
    past = page_table.shape[1] * pg
    L = past + T
    topk = min(TOPK_MAX, L // 4)
    bidx = jnp.arange(Bd)[:, None, None]
    k_idx_past = cache_idx[layer, page_table].reshape(Bd, past, IDX_DIM)
    k_idx_all = jnp.concatenate([k_idx_past.astype(jnp.float32), k_idx_new.astype(jnp.float32)], axis=1)
    qpos = past + jnp.arange(T)
    score = indexer_scores(q_idx, w_idx, k_idx_all)
    score = jnp.where(jnp.arange(L)[None, None, :] <= qpos[None, :, None], score, -jnp.inf)
    _, sel = lax.top_k(score, topk)
    valid = sel <= qpos[None, :, None]
    in_past = sel < past
    sp = jnp.where(in_past, sel, 0)
    phys = page_table[bidx, sp // pg]
    off = sp % pg
    sn = jnp.where(in_past, 0, sel - past)
    sel_mask = in_past[..., None, None]
    k_sel = jnp.where(sel_mask, cache_k[layer, phys, off].astype(jnp.float32), k_new[bidx, sn].astype(jnp.float32))
    v_sel = jnp.where(sel_mask, cache_v[layer, phys, off].astype(jnp.float32), v_new[bidx, sn].astype(jnp.float32))
    return gqa_over_selected(q, k_sel, v_sel, valid)


def setup_inputs(seed: int = 0) -> dict:
    key = jax.random.key(seed)
    ks = jax.random.split(key, 20)
    n_pages = PAST_LEN // PAGE_SIZE
    n_used = DEC_BATCH * n_pages
    n_phys = n_used + n_used // 4

    def nrm(k, shape, scale=1.0):
        return scale * jax.random.normal(k, shape, jnp.float32)

    page_table = jax.random.permutation(ks[7], n_phys)[:n_used].reshape(DEC_BATCH, n_pages).astype(jnp.int32)
    return {
        "x_prompt": nrm(ks[0], (BATCH, SEQ, D_MODEL)),
        "x_sample": nrm(ks[1], (DEC_BATCH, DEC_SEQ, D_MODEL)),
        "cache_sb_k": nrm(ks[2], (DEPTH, n_phys, PAGE_SIZE, SB_HEADS, HEAD_DIM)),
        "cache_sb_v": nrm(ks[3], (DEPTH, n_phys, PAGE_SIZE, SB_HEADS, HEAD_DIM)),
        "cache_dsa_k": nrm(ks[4], (DEPTH, n_phys, PAGE_SIZE, DSA_KV_HEADS, HEAD_DIM)),
        "cache_dsa_v": nrm(ks[5], (DEPTH, n_phys, PAGE_SIZE, DSA_KV_HEADS, HEAD_DIM)),
        "cache_idx_k": nrm(ks[6], (DEPTH, n_phys, PAGE_SIZE, IDX_DIM)),
        "page_table": page_table,
        "g_mix": 1.0 + nrm(ks[8], (DEPTH, D_MODEL), 0.02),
        "w_in": nrm(ks[9], (DEPTH, D_MODEL, IN_W), D_MODEL ** -0.5),
        "w_pa": nrm(ks[10], (DEPTH, SB_W, D_MODEL), SB_W ** -0.5),
        "w_pb": nrm(ks[11], (DEPTH, DSA_Q_W, D_MODEL), DSA_Q_W ** -0.5),
        "w_o": nrm(ks[12], (DEPTH, D_MODEL, D_MODEL), D_MODEL ** -0.5),
        "g_ffn": 1.0 + nrm(ks[13], (DEPTH, D_MODEL), 0.02),
        "w_gate": nrm(ks[14], (DEPTH, D_MODEL, D_FF), D_MODEL ** -0.5),
        "w_up": nrm(ks[15], (DEPTH, D_MODEL, D_FF), D_MODEL ** -0.5),
        "w_down": nrm(ks[16], (DEPTH, D_FF, D_MODEL), D_FF ** -0.5),
        "g_final": 1.0 + nrm(ks[17], (D_MODEL,), 0.02),
    }


def reference(x_prompt, x_sample, cache_sb_k, cache_sb_v, cache_dsa_k, cache_dsa_v, cache_idx_k, page_table,
              g_mix, w_in, w_pa, w_pb, w_o, g_ffn, w_gate, w_up, w_down, g_final):
    B, S, _ = x_prompt.shape
    Bd, T, _ = x_sample.shape
    past = page_table.shape[1] * cache_sb_k.shape[2]
    pos_p = jnp.arange(S)
    pos_s = past + jnp.arange(T)
    xp, xs = x_prompt, x_sample
    p_sbk, p_sbv, p_dk, p_dv, p_ik = [], [], [], [], []
    s_sbk, s_sbv, s_dk, s_dv, s_ik = [], [], [], [], []
    for layer in range(DEPTH):
        out_w = (w_pa[layer], w_pb[layer], w_o[layer], g_ffn[layer], w_gate[layer], w_up[layer], w_down[layer])
        qa, ka, va, qb, kb, vb, qi, ki, wi, ga, gb = layer_inputs(xp, pos_p, g_mix[layer], w_in[layer])
        oa = sb_prompt(qa, ka, va)
        ob = dsa_prompt(qb, kb, vb, qi, ki, wi)
        xp = layer_outputs(xp, oa, ob, ga, gb, *out_w)
        n_pg = S // PAGE_SIZE
        p_sbk.append(ka.reshape(B, n_pg, PAGE_SIZE, SB_HEADS, HEAD_DIM))
        p_sbv.append(va.reshape(B, n_pg, PAGE_SIZE, SB_HEADS, HEAD_DIM))
        p_dk.append(kb.reshape(B, n_pg, PAGE_SIZE, DSA_KV_HEADS, HEAD_DIM))
        p_dv.append(vb.reshape(B, n_pg, PAGE_SIZE, DSA_KV_HEADS, HEAD_DIM))
        p_ik.append(ki.reshape(B, n_pg, PAGE_SIZE, IDX_DIM))
        qa, ka, va, qb, kb, vb, qi, ki, wi, ga, gb = layer_inputs(xs, pos_s, g_mix[layer], w_in[layer])
        oa = sb_sample(qa, ka, va, cache_sb_k, cache_sb_v, page_table, layer)
        ob = dsa_sample(qb, kb, vb, qi, ki, wi, cache_dsa_k, cache_dsa_v, cache_idx_k, page_table, layer)
        xs = layer_outputs(xs, oa, ob, ga, gb, *out_w)
        s_sbk.append(ka)
        s_sbv.append(va)
        s_dk.append(kb)
        s_dv.append(vb)
        s_ik.append(ki)
    y_prompt = rmsnorm(xp, g_final)
    y_sample = rmsnorm(xs, g_final)
    new_sb_k_prompt = jnp.stack(p_sbk)
    new_sb_v_prompt = jnp.stack(p_sbv)
    new_dsa_k_prompt = jnp.stack(p_dk)
    new_dsa_v_prompt = jnp.stack(p_dv)
    new_idx_k_prompt = jnp.stack(p_ik)
    new_sb_k_sample = jnp.stack(s_sbk)
    new_sb_v_sample = jnp.stack(s_sbv)
    new_dsa_k_sample = jnp.stack(s_dk)
    new_dsa_v_sample = jnp.stack(s_dv)
    new_idx_k_sample = jnp.stack(s_ik)
    return (y_prompt, y_sample, new_sb_k_prompt, new_sb_v_prompt, new_dsa_k_prompt, new_dsa_v_prompt, new_idx_k_prompt,
            new_sb_k_sample, new_sb_v_sample, new_dsa_k_sample, new_dsa_v_sample, new_idx_k_sample)
```

```python
import functools

import jax
import jax.numpy as jnp
import numpy as np
from jax import lax
from jax.experimental import pallas as pl
from jax.experimental.pallas import tpu as pltpu

HEAD_DIM = 64
SB_HEADS = 8
DSA_HEADS = 8
DSA_KV_HEADS = 4
IDX_HEADS = 8
IDX_DIM = 32
TOPK_MAX = 256
ROPE_THETA = 500000.0
ROPE_FRACTION = 4
RMS_EPS = 1e-6

SB_W = SB_HEADS * HEAD_DIM
DSA_Q_W = DSA_HEADS * HEAD_DIM
DSA_KV_W = DSA_KV_HEADS * HEAD_DIM
IDX_Q_W = IDX_HEADS * IDX_DIM

LANES = 128
KEY_BLOCK = 128
INT_MIN = -(2 ** 31)
SB_DEAD_LOG = -104.0
NEG_BIG = -1e30
VMEM_LIMIT = 56 * 1024 * 1024

F32 = jnp.float32
BF16 = jnp.bfloat16

_NT = (((1,), (1,)), ((), ()))


def _dot(a, b):
    return jnp.dot(a, b, preferred_element_type=F32)


def _dot_nt(a, b):
    return lax.dot_general(a, b, _NT, preferred_element_type=F32)


def _lane_iota(shape):
    return lax.broadcasted_iota(jnp.int32, shape, len(shape) - 1)


def _row_iota(shape):
    return lax.broadcasted_iota(jnp.int32, shape, len(shape) - 2)


def _softplus(z):
    return jnp.maximum(z, 0.0) + jnp.log1p(jnp.exp(-jnp.abs(z)))


def _split_bf16(x):
    hi = x.astype(BF16)
    lo = (x - hi.astype(F32)).astype(BF16)
    return hi, lo


def _tri(n, lower):
    r = lax.broadcasted_iota(jnp.int32, (n, n), 0)
    c = lax.broadcasted_iota(jnp.int32, (n, n), 1)
    m = (r >= c) if lower else (r <= c)
    return jnp.where(m, 1.0, 0.0).astype(BF16)


def _sort_key(score):
    score = jnp.where(score == 0.0, 0.0, score)
    bits = pltpu.bitcast(score, jnp.int32)
    return bits ^ ((bits >> 31) & 0x7FFFFFFF)


_C_QA = 0
_C_KA = _C_QA + SB_W
_C_VA = _C_KA + SB_W
_C_QB = _C_VA + SB_W
_C_KB = _C_QB + DSA_Q_W
_C_VB = _C_KB + DSA_KV_W
_C_QI = _C_VB + DSA_KV_W
_C_KW = _C_QI + IDX_Q_W
_C_GA = _C_KW + LANES


def _rope(x, c, s, head_w):
    half = head_w // ROPE_FRACTION // 2
    first = (_lane_iota(c.shape) & (head_w - 1)) < half
    tiles = []
    for k in range(x.shape[-1] // LANES):
        xt = x[:, k * LANES:(k + 1) * LANES]
        rot = jnp.where(first, pltpu.roll(xt, LANES - half, 1), pltpu.roll(xt, half, 1))
        tiles.append(xt * c + rot * s)
    return tiles[0] if len(tiles) == 1 else jnp.concatenate(tiles, axis=1)


def _proj_kernel(x_ref, g_ref, w_ref, c64_ref, s64_ref, c32_ref, s32_ref,
                 ka_ref, va_ref, kb_ref, vb_ref, ki_ref,
                 qa16_ref, ka16_ref, va16_ref, qb16_ref, kb16_ref, vb16_ref, qi16_ref, ki16_ref,
                 wi_ref, ga_ref, gb_ref, *, d_model):
    x = x_ref[...]
    y = x * lax.rsqrt(jnp.mean(x * x, axis=-1, keepdims=True) + RMS_EPS)
    h = (y * g_ref[...]).astype(BF16)
    scale = HEAD_DIM ** -0.5

    def cols(lo, width):
        return _dot(h, w_ref[:, lo:lo + width])

    qa = cols(_C_QA, SB_W)
    qa16_ref[...] = (qa * scale).astype(BF16)
    ka = cols(_C_KA, SB_W)
    ka_ref[...] = ka
    ka16_ref[...] = ka.astype(BF16)
    va = cols(_C_VA, SB_W)
    va_ref[...] = va
    va16_ref[...] = va.astype(BF16)

    c64, s64 = c64_ref[...], s64_ref[...]
    qb = _rope(cols(_C_QB, DSA_Q_W), c64, s64, HEAD_DIM)
    qb16_ref[...] = (qb * scale).astype(BF16)
    kb = _rope(cols(_C_KB, DSA_KV_W), c64, s64, HEAD_DIM)
    kb_ref[...] = kb
    kb16_ref[...] = kb.astype(BF16)
    vb = cols(_C_VB, DSA_KV_W)
    vb_ref[...] = vb
    vb16_ref[...] = vb.astype(BF16)

    c32, s32 = c32_ref[...], s32_ref[...]
    qi = _rope(cols(_C_QI, IDX_Q_W), c32, s32, IDX_DIM)
    qi16_ref[...] = qi.astype(BF16)
    kw = cols(_C_KW, LANES)
    ki = _rope(kw, c32, s32, IDX_DIM)[:, :IDX_DIM]
    ki_ref[...] = ki
    ki16_ref[...] = ki.astype(BF16)
    wi_ref[...] = kw[:, IDX_DIM:IDX_DIM + IDX_HEADS] * (IDX_HEADS ** -0.5)

    ga_ref[...] = jax.nn.sigmoid(cols(_C_GA, d_model))
    gb_ref[...] = jax.nn.sigmoid(cols(_C_GA + d_model, d_model))


def _rope_tables(pos, head_w):
    half = head_w // ROPE_FRACTION // 2
    inv_freq = ROPE_THETA ** (-jnp.arange(half, dtype=F32) / half)
    ang = pos.astype(F32)[:, None] * inv_freq[None, :]
    cos, sin = jnp.cos(ang), jnp.sin(ang)
    n = pos.shape[0]
    rest = head_w - 2 * half
    c = jnp.concatenate([cos, cos, jnp.ones((n, rest), F32)], axis=1)
    s = jnp.concatenate([-sin, sin, jnp.zeros((n, rest), F32)], axis=1)
    reps = LANES // head_w
    return jnp.tile(c, (1, reps)), jnp.tile(s, (1, reps))


def _pack_w_in(w_in):
    d = w_in.shape[0]
    pad = jnp.zeros((d, LANES - IDX_DIM - IDX_HEADS), w_in.dtype)
    kw_end = _C_KW + IDX_DIM + IDX_HEADS
    return jnp.concatenate([w_in[:, :_C_KW], w_in[:, _C_KW:kw_end], pad, w_in[:, kw_end:]], axis=1).astype(BF16)


def _proj(x2d, pos, g_mix, w_packed, *, tm):
    n, d = x2d.shape
    p = pos.shape[0]
    tm = min(tm, p)
    assert n % tm == 0 and p % tm == 0
    c64, s64 = _rope_tables(pos, HEAD_DIM)
    c32, s32 = _rope_tables(pos, IDX_DIM)
    pblocks = p // tm
    row = lambda w: pl.BlockSpec((tm, w), lambda i: (i, 0))
    tab = pl.BlockSpec((tm, LANES), lambda i: (i % pblocks, 0))
    const = lambda shape: pl.BlockSpec(shape, lambda i: (0, 0))
    widths_f32 = [SB_W, SB_W, DSA_KV_W, DSA_KV_W, IDX_DIM]
    widths_b16 = [SB_W, SB_W, SB_W, DSA_Q_W, DSA_KV_W, DSA_KV_W, IDX_Q_W, IDX_DIM]
    out_shape = ([jax.ShapeDtypeStruct((n, w), F32) for w in widths_f32]
                 + [jax.ShapeDtypeStruct((n, w), BF16) for w in widths_b16]
                 + [jax.ShapeDtypeStruct((n, IDX_HEADS), F32),
                    jax.ShapeDtypeStruct((n, d), F32), jax.ShapeDtypeStruct((n, d), F32)])
    out_specs = [row(w) for w in widths_f32 + widths_b16] + [row(IDX_HEADS), row(d), row(d)]
    return pl.pallas_call(
        functools.partial(_proj_kernel, d_model=d),
        grid=(n // tm,),
        in_specs=[row(d), const((1, d)), const(w_packed.shape), tab, tab, tab, tab],
        out_specs=out_specs,
        out_shape=out_shape,
        compiler_params=pltpu.CompilerParams(dimension_semantics=("arbitrary",), vmem_limit_bytes=VMEM_LIMIT),
        name="proj",
    )(x2d, g_mix.reshape(1, d), w_packed, c64, s64, c32, s32)


def _sb_block(qm, k_tile, v_tile, acc, tri, mask):
    z = _dot_nt(qm, k_tile)
    l = -_softplus(z)
    if mask is not None:
        l = jnp.where(mask, l, 0.0)
    l_hi, l_lo = _split_bf16(l)
    cs = _dot(l_hi, tri) + _dot(l_lo, tri)
    e = z + cs + acc
    a = jnp.exp(e)
    if mask is not None:
        a = jnp.where(mask, a, 0.0)
    return _dot(a.astype(BF16), v_tile), acc + cs[:, 0:1]


def _sb_prompt_kernel(q_ref, k_ref, v_ref, o_ref, acc_ref, out_ref, *, tq):
    i = pl.program_id(1)
    n_pairs = SB_HEADS // 2
    tri = _tri(tq, lower=True)
    lane = _lane_iota((tq, LANES))
    diag_mask = _lane_iota((tq, tq)) < _row_iota((tq, tq))

    def qm(h):
        p, r = divmod(h, 2)
        qt = q_ref[0, :, p * LANES:(p + 1) * LANES]
        keep = (lane < HEAD_DIM) if r == 0 else (lane >= HEAD_DIM)
        return jnp.where(keep, qt, jnp.zeros_like(qt))

    def do_block(j, mask, first):
        start = pl.multiple_of(j * tq, tq)
        worst = None
        for h in range(SB_HEADS):
            p = h // 2
            k_tile = k_ref[0, pl.ds(start, tq), p * LANES:(p + 1) * LANES]
            v_tile = v_ref[0, pl.ds(start, tq), p * LANES:(p + 1) * LANES]
            acc = jnp.zeros((tq, 1), F32) if first else acc_ref[h]
            contrib, acc = _sb_block(qm(h), k_tile, v_tile, acc, tri, mask)
            acc_ref[h] = acc
            if first:
                out_ref[h] = contrib
            else:
                out_ref[h] += contrib
            worst = acc if worst is None else jnp.maximum(worst, acc)
        return (jnp.max(worst) > SB_DEAD_LOG).astype(jnp.int32)

    go0 = do_block(i, diag_mask, True)

    def cond(c):
        j, go = c
        return jnp.logical_and(j >= 0, go > 0)

    def body(c):
        j, _ = c
        return j - 1, do_block(j, None, False)

    lax.while_loop(cond, body, (i - 1, go0))

    for p in range(n_pairs):
        o_ref[0, :, p * LANES:(p + 1) * LANES] = jnp.where(
            lane < HEAD_DIM, out_ref[2 * p], out_ref[2 * p + 1]).astype(o_ref.dtype)


def _whole_seq_spec(s, w):
    return pl.BlockSpec((1, s, w), lambda b, i: (b, 0, 0), pipeline_mode=pl.Buffered(1))


def _sb_prompt(qa16, ka16, va16, *, tq=KEY_BLOCK):
    b, s, w = qa16.shape
    assert s % tq == 0 and w == SB_W
    return pl.pallas_call(
        functools.partial(_sb_prompt_kernel, tq=tq),
        grid=(b, s // tq),
        in_specs=[pl.BlockSpec((1, tq, w), lambda bi, i: (bi, i, 0)), _whole_seq_spec(s, w), _whole_seq_spec(s, w)],
        out_specs=pl.BlockSpec((1, tq, w), lambda bi, i: (bi, i, 0)),
        out_shape=jax.ShapeDtypeStruct((b, s, w), BF16),
        scratch_shapes=[pltpu.VMEM((SB_HEADS, tq, 1), F32), pltpu.VMEM((SB_HEADS, tq, LANES), F32)],
        compiler_params=pltpu.CompilerParams(dimension_semantics=("arbitrary", "arbitrary"),
                                             vmem_limit_bytes=VMEM_LIMIT),
        name="sb_prompt",
    )(qa16, ka16, va16)


def _topk_threshold(count_ge, topk, rows):
    def bit_body(it, u):
        bit = lax.shift_left(jnp.int32(1), 31 - it)
        cand = u | bit
        c = count_ge(cand ^ INT_MIN)
        return jnp.where(c >= topk, cand, u)

    u = lax.fori_loop(0, 32, bit_body, jnp.zeros((rows, 1), jnp.int32))
    return u ^ INT_MIN


def _select(key, tau, need, run, tri_prefix):
    eq = key == tau
    ties = jnp.where(eq, 1.0, 0.0)
    pref = _dot(ties.astype(BF16), tri_prefix) + run
    sel = jnp.logical_or(key > tau, jnp.logical_and(eq, pref <= need))
    return sel, run + jnp.sum(ties, axis=-1, keepdims=True)


def _softmax_step(z, sel, m, l, acc, v_tile):
    zm = jnp.where(sel, z, NEG_BIG)
    m_new = jnp.maximum(m, jnp.max(zm, axis=-1, keepdims=True))
    alpha = jnp.exp(m - m_new)
    p = jnp.where(sel, jnp.exp(z - m_new), 0.0)
    l_new = alpha * l + jnp.sum(p, axis=-1, keepdims=True)
    acc_new = alpha * acc + _dot(p.astype(BF16), v_tile)
    return m_new, l_new, acc_new


def _dsa_prompt_kernel(qb_ref, qi_ref, wi_ref, kb_ref, vb_ref, ki_ref, o_ref,
                       keys_ref, qs_ref, wb_ref, qg_ref, m_ref, l_ref, acc_ref, *, tq, topk):
    i = pl.program_id(1)
    tk = KEY_BLOCK
    lane = _lane_iota((tq, LANES))
    q_pos = i * tq + _row_iota((tq, tk))
    idx_scale = IDX_DIM ** -0.5

    for h in range(IDX_HEADS):
        qs_ref[h * tq:(h + 1) * tq, :] = qi_ref[0, :, h * IDX_DIM:(h + 1) * IDX_DIM].astype(F32).astype(BF16)
        wb_ref[h] = jnp.broadcast_to(wi_ref[0, :, h:h + 1], (tq, LANES))
    for g in range(DSA_KV_HEADS):
        qt = qb_ref[0, :, g * LANES:(g + 1) * LANES].astype(F32)
        qr = pltpu.roll(qt, HEAD_DIM, 1)
        if g % 2 == 0:
            r0, r1 = jnp.where(lane < HEAD_DIM, qt, 0.0), jnp.where(lane < HEAD_DIM, qr, 0.0)
        else:
            r0, r1 = jnp.where(lane >= HEAD_DIM, qr, 0.0), jnp.where(lane >= HEAD_DIM, qt, 0.0)
        qg_ref[g, 0:tq, :] = r0.astype(BF16)
        qg_ref[g, tq:2 * tq, :] = r1.astype(BF16)

    def score_body(j, carry):
        start = pl.multiple_of(j * tk, tk)
        sc = _dot_nt(qs_ref[...], ki_ref[0, pl.ds(start, tk), :])
        tot = jnp.zeros((tq, tk), F32)
        for h in range(IDX_HEADS):
            tot = tot + jnp.maximum(sc[h * tq:(h + 1) * tq, :] * idx_scale, 0.0) * wb_ref[h]
        causal = (j * tk + _lane_iota((tq, tk))) <= q_pos
        keys_ref[j] = jnp.where(causal, _sort_key(tot), INT_MIN)
        return carry

    lax.fori_loop(0, i + 1, score_body, 0)

    def count(pred):
        def body(j, c):
            return c + jnp.where(pred(keys_ref[j]), 1.0, 0.0)
        c = lax.fori_loop(0, i + 1, body, jnp.zeros((tq, tk), F32))
        return jnp.sum(c, axis=-1, keepdims=True)

    tau = _topk_threshold(lambda v: count(lambda key: key >= v), float(topk), tq)
    need = float(topk) - count(lambda key: key > tau)

    m_ref[...] = jnp.full(m_ref.shape, NEG_BIG, F32)
    l_ref[...] = jnp.zeros(l_ref.shape, F32)
    acc_ref[...] = jnp.zeros(acc_ref.shape, F32)
    tri_prefix = _tri(tk, lower=False)

    def attn_body(j, run):
        start = pl.multiple_of(j * tk, tk)
        key = keys_ref[j]
        sel, run = _select(key, tau, need, run, tri_prefix)
        causal = (j * tk + _lane_iota((tq, tk))) <= q_pos
        sel = jnp.logical_and(sel, causal)
        sel2 = jnp.concatenate([sel, sel], axis=0)
        for g in range(DSA_KV_HEADS):
            t = g // 2
            k_tile = kb_ref[0, pl.ds(start, tk), t * LANES:(t + 1) * LANES]
            v_tile = vb_ref[0, pl.ds(start, tk), t * LANES:(t + 1) * LANES]
            z = _dot_nt(qg_ref[g], k_tile)
            m_ref[g], l_ref[g], acc_ref[g] = _softmax_step(z, sel2, m_ref[g], l_ref[g], acc_ref[g], v_tile)
        return run

    lax.fori_loop(0, i + 1, attn_body, jnp.zeros((tq, 1), F32))

    for g in range(DSA_KV_HEADS):
        o = acc_ref[g] / l_ref[g]
        a0, a1 = o[0:tq], o[tq:2 * tq]
        if g % 2 == 0:
            tile = jnp.where(lane < HEAD_DIM, a0, pltpu.roll(a1, HEAD_DIM, 1))
        else:
            tile = jnp.where(lane < HEAD_DIM, pltpu.roll(a0, HEAD_DIM, 1), a1)
        o_ref[0, :, g * LANES:(g + 1) * LANES] = tile.astype(o_ref.dtype)


def _dsa_prompt(qb16, qi16, wi, kb16, vb16, ki16, *, tq=KEY_BLOCK):
    b, s, _ = qb16.shape
    assert s % tq == 0 and tq == KEY_BLOCK
    topk = min(TOPK_MAX, s // 4)
    qblk = lambda w: pl.BlockSpec((1, tq, w), lambda bi, i: (bi, i, 0))
    return pl.pallas_call(
        functools.partial(_dsa_prompt_kernel, tq=tq, topk=topk),
        grid=(b, s // tq),
        in_specs=[qblk(DSA_Q_W), qblk(IDX_Q_W), qblk(IDX_HEADS),
                  _whole_seq_spec(s, DSA_KV_W), _whole_seq_spec(s, DSA_KV_W), _whole_seq_spec(s, IDX_DIM)],
        out_specs=qblk(DSA_Q_W),
        out_shape=jax.ShapeDtypeStruct((b, s, DSA_Q_W), BF16),
        scratch_shapes=[
            pltpu.VMEM((s // KEY_BLOCK, tq, KEY_BLOCK), jnp.int32),
            pltpu.VMEM((IDX_HEADS * tq, IDX_DIM), BF16),
            pltpu.VMEM((IDX_HEADS, tq, LANES), F32),
            pltpu.VMEM((DSA_KV_HEADS, 2 * tq, LANES), BF16),
            pltpu.VMEM((DSA_KV_HEADS, 2 * tq, 1), F32),
            pltpu.VMEM((DSA_KV_HEADS, 2 * tq, 1), F32),
            pltpu.VMEM((DSA_KV_HEADS, 2 * tq, LANES), F32),
        ],
        compiler_params=pltpu.CompilerParams(dimension_semantics=("arbitrary", "arbitrary"),
                                             vmem_limit_bytes=VMEM_LIMIT),
        name="dsa_prompt",
    )(qb16, qi16, wi, kb16, vb16, ki16)


def _sb_sample_kernel(pt_ref, q_ref, kn_ref, vn_ref, kc_ref, vc_ref, o_ref,
                      acc_ref, out_ref, kpad_ref, vpad_ref, go_ref, *, t, n_pages):
    step = pl.program_id(1)
    rows = SB_HEADS * t
    tri = _tri(KEY_BLOCK, lower=True)

    def process(k16, v16, acc, mask):
        z = _dot_nt(q_ref[0], k16)
        l = -_softplus(z)
        if mask is not None:
            l = jnp.where(mask, l, 0.0)
        l_hi, l_lo = _split_bf16(l)
        cs = _dot(l_hi, tri) + _dot(l_lo, tri)
        a = jnp.exp(z + cs + acc)
        if mask is not None:
            a = jnp.where(mask, a, 0.0)
        return _dot(a.astype(BF16), v16), acc + cs[:, 0:1]

    @pl.when(step == 0)
    def _():
        kpad_ref[...] = jnp.zeros(kpad_ref.shape, BF16)
        vpad_ref[...] = jnp.zeros(vpad_ref.shape, BF16)
        kpad_ref[0:t, :] = kn_ref[0]
        vpad_ref[0:t, :] = vn_ref[0]
        q_time = _row_iota((rows, KEY_BLOCK)) & (t - 1)
        mask = _lane_iota((rows, KEY_BLOCK)) < q_time
        contrib, acc = process(kpad_ref[...], vpad_ref[...], jnp.zeros((rows, 1), F32), mask)
        out_ref[...] = contrib
        acc_ref[...] = acc
        go_ref[0] = (jnp.max(acc) > SB_DEAD_LOG).astype(jnp.int32)

    @pl.when(jnp.logical_and(step > 0, go_ref[0] > 0))
    def _():
        contrib, acc = process(kc_ref[0].astype(BF16), vc_ref[0].astype(BF16), acc_ref[...], None)
        out_ref[...] += contrib
        acc_ref[...] = acc
        go_ref[0] = (jnp.max(acc) > SB_DEAD_LOG).astype(jnp.int32)

    @pl.when(step == n_pages)
    def _():
        lane = _lane_iota((t, SB_W))
        res = jnp.zeros((t, SB_W), F32)
        for h in range(SB_HEADS):
            res = res + jnp.where(lane // HEAD_DIM == h, out_ref[h * t:(h + 1) * t, :], 0.0)
        o_ref[0] = res.astype(o_ref.dtype)


def _head_rows(q, n_groups, group_of_head):
    bd, t, heads, hd = q.shape
    onehot = np.zeros((heads, n_groups), np.float32)
    for h in range(heads):
        onehot[h, group_of_head(h)] = 1.0
    qt = jnp.transpose(q, (0, 2, 1, 3))
    out = qt[:, :, :, None, :] * jnp.asarray(onehot, q.dtype)[None, :, None, :, None]
    return out.reshape(bd, heads * t, n_groups * hd)


def _sb_sample(qa16, ka16, va16, cache_k, cache_v, page_table):
    bd, t, w = qa16.shape
    n_pages = page_table.shape[1]
    page = cache_k.shape[1]
    assert page == KEY_BLOCK and t & (t - 1) == 0 and t <= KEY_BLOCK
    q_rows = _head_rows(qa16.reshape(bd, t, SB_HEADS, HEAD_DIM), SB_HEADS, lambda h: h)
    rows = SB_HEADS * t

    def page_map(b, s, pt):
        return (pt[b, n_pages - jnp.maximum(s, 1)], 0, 0)

    per_b = lambda r, c: pl.BlockSpec((1, r, c), lambda b, s, pt: (b, 0, 0))
    return pl.pallas_call(
        functools.partial(_sb_sample_kernel, t=t, n_pages=n_pages),
        grid_spec=pltpu.PrefetchScalarGridSpec(
            num_scalar_prefetch=1,
            grid=(bd, n_pages + 1),
            in_specs=[per_b(rows, w), per_b(t, w), per_b(t, w),
                      pl.BlockSpec((1, page, w), page_map), pl.BlockSpec((1, page, w), page_map)],
            out_specs=per_b(t, w),
            scratch_shapes=[pltpu.VMEM((rows, 1), F32), pltpu.VMEM((rows, w), F32),
                            pltpu.VMEM((KEY_BLOCK, w), BF16), pltpu.VMEM((KEY_BLOCK, w), BF16),
                            pltpu.SMEM((1,), jnp.int32)],
        ),
        out_shape=jax.ShapeDtypeStruct((bd, t, w), BF16),
        compiler_params=pltpu.CompilerParams(dimension_semantics=("arbitrary", "arbitrary"),
                                             vmem_limit_bytes=VMEM_LIMIT),
        name="sb_sample",
    )(page_table, q_rows, ka16, va16, cache_k, cache_v)


def _dsa_scores_kernel(pt_ref, q_ref, w_ref, kn_ref, kc_ref, keys_ref, tau_ref, need_ref, kpad_ref,
                       *, t, n_pages, topk):
    step = pl.program_id(1)
    idx_scale = IDX_DIM ** -0.5

    def scores(k16):
        sc = _dot_nt(q_ref[0], k16)
        rel = jnp.maximum(sc * idx_scale, 0.0) * w_ref[0]
        tot = jnp.zeros((t, KEY_BLOCK), F32)
        for h in range(IDX_HEADS):
            tot = tot + rel[h * t:(h + 1) * t, :]
        return _sort_key(tot)

    @pl.when(step < n_pages)
    def _():
        keys_ref[0, step] = scores(kc_ref[0].astype(BF16))

    @pl.when(step == n_pages)
    def _():
        kpad_ref[...] = jnp.zeros(kpad_ref.shape, BF16)
        kpad_ref[0:t, :] = kn_ref[0]
        causal = _lane_iota((t, KEY_BLOCK)) <= _row_iota((t, KEY_BLOCK))
        keys_ref[0, n_pages] = jnp.where(causal, scores(kpad_ref[...]), INT_MIN)

        def count(pred):
            def body(j, c):
                return c + jnp.where(pred(keys_ref[0, j]), 1.0, 0.0)
            c = lax.fori_loop(0, n_pages + 1, body, jnp.zeros((t, KEY_BLOCK), F32))
            return jnp.sum(c, axis=-1, keepdims=True)

        tau = _topk_threshold(lambda v: count(lambda key: key >= v), float(topk), t)
        tau_ref[0] = tau
        need_ref[0] = float(topk) - count(lambda key: key > tau)


def _dsa_attn_kernel(pt_ref, q_ref, kn_ref, vn_ref, keys_ref, tau_ref, need_ref, kc_ref, vc_ref, o_ref,
                     m_ref, l_ref, acc_ref, run_ref, kpad_ref, vpad_ref, *, t, n_pages):
    step = pl.program_id(1)
    rows = DSA_HEADS * t
    tri_prefix = _tri(KEY_BLOCK, lower=False)

    @pl.when(step == 0)
    def _():
        m_ref[...] = jnp.full(m_ref.shape, NEG_BIG, F32)
        l_ref[...] = jnp.zeros(l_ref.shape, F32)
        acc_ref[...] = jnp.zeros(acc_ref.shape, F32)
        run_ref[...] = jnp.zeros(run_ref.shape, F32)

    def process(k16, v16, valid):
        sel, run = _select(keys_ref[0, 0], tau_ref[0], need_ref[0], run_ref[...], tri_prefix)
        run_ref[...] = run
        if valid is not None:
            sel = jnp.logical_and(sel, valid)
        sel_rows = jnp.concatenate([sel] * DSA_HEADS, axis=0)
        z = _dot_nt(q_ref[0], k16)
        m_ref[...], l_ref[...], acc_ref[...] = _softmax_step(z, sel_rows, m_ref[...], l_ref[...], acc_ref[...], v16)

    @pl.when(step < n_pages)
    def _():
        process(kc_ref[0].astype(BF16), vc_ref[0].astype(BF16), None)

    @pl.when(step == n_pages)
    def _():
        kpad_ref[...] = jnp.zeros(kpad_ref.shape, BF16)
        vpad_ref[...] = jnp.zeros(vpad_ref.shape, BF16)
        kpad_ref[0:t, :] = kn_ref[0]
        vpad_ref[0:t, :] = vn_ref[0]
        causal = _lane_iota((t, KEY_BLOCK)) <= _row_iota((t, KEY_BLOCK))
        process(kpad_ref[...], vpad_ref[...], causal)
        o = acc_ref[...] / l_ref[...]
        lane = _lane_iota((t, DSA_KV_W))
        for h in range(DSA_HEADS):
            g = h // (DSA_HEADS // DSA_KV_HEADS)
            o_ref[0, h] = jnp.where(lane // HEAD_DIM == g, o[h * t:(h + 1) * t, :], 0.0)


def _dsa_sample(qb16, kb16, vb16, qi16, ki16, wi, cache_k, cache_v, cache_idx, page_table):
    bd, t, _ = qb16.shape
    n_pages = page_table.shape[1]
    page = cache_k.shape[1]
    assert page == KEY_BLOCK and t % 8 == 0
    topk = min(TOPK_MAX, (n_pages * page + t) // 4)
    nb = n_pages + 1
    rep = DSA_HEADS // DSA_KV_HEADS

    qi_rows = jnp.transpose(qi16.reshape(bd, t, IDX_HEADS, IDX_DIM), (0, 2, 1, 3)).reshape(bd, IDX_HEADS * t, IDX_DIM)
    wi_rows = jnp.transpose(wi, (0, 2, 1)).reshape(bd, IDX_HEADS * t, 1)
    qb_rows = _head_rows(qb16.reshape(bd, t, DSA_HEADS, HEAD_DIM), DSA_KV_HEADS, lambda h: h // rep)

    per_b = lambda r, c: pl.BlockSpec((1, r, c), lambda b, s, pt: (b, 0, 0))

    def page_map(b, s, pt):
        return (pt[b, jnp.minimum(s, n_pages - 1)], 0, 0)

    cparams = pltpu.CompilerParams(dimension_semantics=("arbitrary", "arbitrary"), vmem_limit_bytes=VMEM_LIMIT)
    keys, tau, need = pl.pallas_call(
        functools.partial(_dsa_scores_kernel, t=t, n_pages=n_pages, topk=topk),
        grid_spec=pltpu.PrefetchScalarGridSpec(
            num_scalar_prefetch=1,
            grid=(bd, nb),
            in_specs=[per_b(IDX_HEADS * t, IDX_DIM), per_b(IDX_HEADS * t, 1), per_b(t, IDX_DIM),
                      pl.BlockSpec((1, page, IDX_DIM), page_map)],
            out_specs=[pl.BlockSpec((1, nb, t, KEY_BLOCK), lambda b, s, pt: (b, 0, 0, 0)), per_b(t, 1), per_b(t, 1)],
            scratch_shapes=[pltpu.VMEM((KEY_BLOCK, IDX_DIM), BF16)],
        ),
        out_shape=[jax.ShapeDtypeStruct((bd, nb, t, KEY_BLOCK), jnp.int32),
                   jax.ShapeDtypeStruct((bd, t, 1), jnp.int32), jax.ShapeDtypeStruct((bd, t, 1), F32)],
        compiler_params=cparams,
        name="dsa_scores",
    )(page_table, qi_rows, wi_rows, ki16, cache_idx)

    rows = DSA_HEADS * t
    o_rows = pl.pallas_call(
        functools.partial(_dsa_attn_kernel, t=t, n_pages=n_pages),
        grid_spec=pltpu.PrefetchScalarGridSpec(
            num_scalar_prefetch=1,
            grid=(bd, nb),
            in_specs=[per_b(rows, DSA_KV_W), per_b(t, DSA_KV_W), per_b(t, DSA_KV_W),
                      pl.BlockSpec((1, 1, t, KEY_BLOCK), lambda b, s, pt: (b, s, 0, 0)), per_b(t, 1), per_b(t, 1),
                      pl.BlockSpec((1, page, DSA_KV_W), page_map), pl.BlockSpec((1, page, DSA_KV_W), page_map)],
            out_specs=pl.BlockSpec((1, DSA_HEADS, t, DSA_KV_W), lambda b, s, pt: (b, 0, 0, 0)),
            scratch_shapes=[pltpu.VMEM((rows, 1), F32), pltpu.VMEM((rows, 1), F32), pltpu.VMEM((rows, DSA_KV_W), F32),
                            pltpu.VMEM((t, 1), F32),
                            pltpu.VMEM((KEY_BLOCK, DSA_KV_W), BF16), pltpu.VMEM((KEY_BLOCK, DSA_KV_W), BF16)],
        ),
        out_shape=jax.ShapeDtypeStruct((bd, DSA_HEADS, t, DSA_KV_W), F32),
        compiler_params=cparams,
        name="dsa_attn",
    )(page_table, qb_rows, kb16, vb16, keys, tau, need, cache_k, cache_v)
    o = o_rows.reshape(bd, DSA_HEADS, t, DSA_KV_HEADS, HEAD_DIM).sum(axis=3)
    return jnp.transpose(o, (0, 2, 1, 3)).reshape(bd, t, DSA_Q_W)


def _out_kernel(x_ref, oa_ref, ob_ref, ga_ref, gb_ref, wpa_ref, wpb_ref, wo_ref, gf_ref, wg_ref, wu_ref, wd_ref,
                gfin_ref, y_ref):
    m = ga_ref[...] * _dot(oa_ref[...], wpa_ref[...]) + gb_ref[...] * _dot(ob_ref[...], wpb_ref[...])
    x = x_ref[...] + _dot(m.astype(BF16), wo_ref[...])
    h = x * lax.rsqrt(jnp.mean(x * x, axis=-1, keepdims=True) + RMS_EPS) * gf_ref[...]
    h16 = h.astype(BF16)
    gate = _dot(h16, wg_ref[...])
    up = _dot(h16, wu_ref[...])
    f = _dot((gate * jax.nn.sigmoid(gate) * up).astype(BF16), wd_ref[...])
    x = x + f
    y_ref[...] = x * lax.rsqrt(jnp.mean(x * x, axis=-1, keepdims=True) + RMS_EPS) * gfin_ref[...]


def _out(x2d, oa16, ob16, ga, gb, w_pa, w_pb, w_o, g_ffn, w_gate, w_up, w_down, g_final, *, tm):
    n, d = x2d.shape
    tm = min(tm, n)
    assert n % tm == 0
    row = lambda w: pl.BlockSpec((tm, w), lambda i: (i, 0))
    const = lambda a: pl.BlockSpec(a.shape, lambda i: (0, 0), pipeline_mode=pl.Buffered(1))
    weights = [w_pa, w_pb, w_o]
    ffn = [w_gate, w_up, w_down]
    gf, gfin = g_ffn.reshape(1, d), g_final.reshape(1, d)
    args = [x2d, oa16, ob16, ga, gb, *weights, gf, *ffn, gfin]
    in_specs = [row(d), row(oa16.shape[1]), row(ob16.shape[1]), row(d), row(d),
                *[const(w) for w in weights], const(gf), *[const(w) for w in ffn], const(gfin)]
    return pl.pallas_call(
        _out_kernel,
        grid=(n // tm,),
        in_specs=in_specs,
        out_specs=row(d),
        out_shape=jax.ShapeDtypeStruct((n, d), F32),
        compiler_params=pltpu.CompilerParams(dimension_semantics=("arbitrary",), vmem_limit_bytes=VMEM_LIMIT),
        name="out",
    )(*args)


def kernel(x_prompt, x_sample, cache_sb_k, cache_sb_v, cache_dsa_k, cache_dsa_v, cache_idx_k, page_table,
           g_mix, w_in, w_pa, w_pb, w_o, g_ffn, w_gate, w_up, w_down, g_final):
    b, s, d = x_prompt.shape
    bd, t, _ = x_sample.shape
    depth, n_phys, page = cache_sb_k.shape[:3]
    assert depth == 1, "single-layer step"
    past = page_table.shape[1] * page
    n_pg = s // page

    w_packed = _pack_w_in(w_in[0])
    pos_p = jnp.arange(s)
    pos_s = past + jnp.arange(bd * t) % t
    xp2 = x_prompt.reshape(b * s, d)
    xs2 = x_sample.reshape(bd * t, d)

    (ka, va, kb, vb, ki, qa16, ka16, va16, qb16, kb16, vb16, qi16, ki16, wi, ga, gb) = _proj(
        xp2, pos_p, g_mix[0], w_packed, tm=256)
    (ska, sva, skb, svb, ski, sqa16, ska16, sva16, sqb16, skb16, svb16, sqi16, ski16, swi, sga, sgb) = _proj(
        xs2, pos_s, g_mix[0], w_packed, tm=256)

    seq = lambda a: a.reshape(b, s, a.shape[-1])
    dec = lambda a: a.reshape(bd, t, a.shape[-1])

    oa = _sb_prompt(seq(qa16), seq(ka16), seq(va16))
    ob = _dsa_prompt(seq(qb16), seq(qi16), seq(wi), seq(kb16), seq(vb16), seq(ki16))
    b16 = lambda w: w[0].astype(BF16)
    out_w = (b16(w_pa), b16(w_pb), b16(w_o), g_ffn[0], b16(w_gate), b16(w_up), b16(w_down), g_final)
    y_prompt = _out(xp2, oa.reshape(b * s, SB_W), ob.reshape(b * s, DSA_Q_W), ga, gb, *out_w, tm=256).reshape(b, s, d)

    soa = _sb_sample(dec(sqa16), dec(ska16), dec(sva16), cache_sb_k[0].reshape(n_phys, page, SB_W),
                     cache_sb_v[0].reshape(n_phys, page, SB_W), page_table)
    sob = _dsa_sample(dec(sqb16), dec(skb16), dec(svb16), dec(sqi16), dec(ski16), dec(swi),
                      cache_dsa_k[0].reshape(n_phys, page, DSA_KV_W), cache_dsa_v[0].reshape(n_phys, page, DSA_KV_W),
                      cache_idx_k[0], page_table)
    y_sample = _out(xs2, soa.reshape(bd * t, SB_W), sob.reshape(bd * t, DSA_Q_W).astype(BF16), sga, sgb, *out_w,
                    tm=256).reshape(bd, t, d)

    return (y_prompt, y_sample,
            ka.reshape(1, b, n_pg, page, SB_HEADS, HEAD_DIM), va.reshape(1, b, n_pg, page, SB_HEADS, HEAD_DIM),
            kb.reshape(1, b, n_pg, page, DSA_KV_HEADS, HEAD_DIM), vb.reshape(1, b, n_pg, page, DSA_KV_HEADS, HEAD_DIM),
            ki.reshape(1, b, n_pg, page, IDX_DIM),
            ska.reshape(1, bd, t, SB_HEADS, HEAD_DIM), sva.reshape(1, bd, t, SB_HEADS, HEAD_DIM),
            skb.reshape(1, bd, t, DSA_KV_HEADS, HEAD_DIM), svb.reshape(1, bd, t, DSA_KV_HEADS, HEAD_DIM),
            ski.reshape(1, bd, t, IDX_DIM))
```

```python
import functools

import jax
import jax.numpy as jnp
import numpy as np
from jax import lax
from jax.experimental import pallas as pl
from jax.experimental.pallas import tpu as pltpu

HEAD_DIM = 64
SB_HEADS = 8
DSA_HEADS = 8
DSA_KV_HEADS = 4
IDX_HEADS = 8
IDX_DIM = 32
TOPK_MAX = 256
ROPE_THETA = 500000.0
ROPE_FRACTION = 4
RMS_EPS = 1e-6

SB_W = SB_HEADS * HEAD_DIM
DSA_Q_W = DSA_HEADS * HEAD_DIM
DSA_KV_W = DSA_KV_HEADS * HEAD_DIM
IDX_Q_W = IDX_HEADS * IDX_DIM

LANES = 128
KEY_BLOCK = 128
INT_MIN = -(2 ** 31)
SB_DEAD_LOG = -104.0
NEG_BIG = -1e30
VMEM_LIMIT = 56 * 1024 * 1024

F32 = jnp.float32
BF16 = jnp.bfloat16

_NT = (((1,), (1,)), ((), ()))


def _dot(a, b):
    return jnp.dot(a, b, preferred_element_type=F32)


def _dot_nt(a, b):
    return lax.dot_general(a, b, _NT, preferred_element_type=F32)


def _lane_iota(shape):
    return lax.broadcasted_iota(jnp.int32, shape, len(shape) - 1)


def _row_iota(shape):
    return lax.broadcasted_iota(jnp.int32, shape, len(shape) - 2)


def _softplus(z):
    return jnp.maximum(z, 0.0) + jnp.log1p(jnp.exp(-jnp.abs(z)))


def _split_bf16(x):
    hi = x.astype(BF16)
    lo = (x - hi.astype(F32)).astype(BF16)
    return hi, lo


def _tri(n, lower):
    r = lax.broadcasted_iota(jnp.int32, (n, n), 0)
    c = lax.broadcasted_iota(jnp.int32, (n, n), 1)
    m = (r >= c) if lower else (r <= c)
    return jnp.where(m, 1.0, 0.0).astype(BF16)


def _sort_key(score):
    score = jnp.where(score == 0.0, 0.0, score)
    bits = pltpu.bitcast(score, jnp.int32)
    return bits ^ ((bits >> 31) & 0x7FFFFFFF)


_C_QA = 0
_C_KA = _C_QA + SB_W
_C_VA = _C_KA + SB_W
_C_QB = _C_VA + SB_W
_C_KB = _C_QB + DSA_Q_W
_C_VB = _C_KB + DSA_KV_W
_C_QI = _C_VB + DSA_KV_W
_C_KW = _C_QI + IDX_Q_W
_C_GA = _C_KW + LANES


def _rope(x, c, s, head_w):
    half = head_w // ROPE_FRACTION // 2
    first = (_lane_iota(c.shape) & (head_w - 1)) < half
    tiles = []
    for k in range(x.shape[-1] // LANES):
        xt = x[:, k * LANES:(k + 1) * LANES]
        rot = jnp.where(first, pltpu.roll(xt, LANES - half, 1), pltpu.roll(xt, half, 1))
        tiles.append(xt * c + rot * s)
    return tiles[0] if len(tiles) == 1 else jnp.concatenate(tiles, axis=1)


def _proj_kernel(x_ref, g_ref, w_ref, c64_ref, s64_ref, c32_ref, s32_ref,
                 ka_ref, va_ref, kb_ref, vb_ref, ki_ref,
                 qa16_ref, ka16_ref, va16_ref, qb16_ref, kb16_ref, vb16_ref, qi16_ref, ki16_ref,
                 wi_ref, ga_ref, gb_ref, *, d_model):
    x = x_ref[...]
    y = x * lax.rsqrt(jnp.mean(x * x, axis=-1, keepdims=True) + RMS_EPS)
    h = (y * g_ref[...]).astype(BF16)
    scale = HEAD_DIM ** -0.5

    def cols(lo, width):
        return _dot(h, w_ref[:, lo:lo + width])

    qa = cols(_C_QA, SB_W)
    qa16_ref[...] = (qa * scale).astype(BF16)
    ka = cols(_C_KA, SB_W)
    ka_ref[...] = ka
    ka16_ref[...] = ka.astype(BF16)
    va = cols(_C_VA, SB_W)
    va_ref[...] = va
    va16_ref[...] = va.astype(BF16)

    c64, s64 = c64_ref[...], s64_ref[...]
    qb = _rope(cols(_C_QB, DSA_Q_W), c64, s64, HEAD_DIM)
    qb16_ref[...] = (qb * scale).astype(BF16)
    kb = _rope(cols(_C_KB, DSA_KV_W), c64, s64, HEAD_DIM)
    kb_ref[...] = kb
    kb16_ref[...] = kb.astype(BF16)
    vb = cols(_C_VB, DSA_KV_W)
    vb_ref[...] = vb
    vb16_ref[...] = vb.astype(BF16)

    c32, s32 = c32_ref[...], s32_ref[...]
    qi = _rope(cols(_C_QI, IDX_Q_W), c32, s32, IDX_DIM)
    qi16_ref[...] = qi.astype(BF16)
    kw = cols(_C_KW, LANES)
    ki = _rope(kw, c32, s32, IDX_DIM)[:, :IDX_DIM]
    ki_ref[...] = ki
    ki16_ref[...] = ki.astype(BF16)
    wi_ref[...] = kw[:, IDX_DIM:IDX_DIM + IDX_HEADS] * (IDX_HEADS ** -0.5)

    ga_ref[...] = jax.nn.sigmoid(cols(_C_GA, d_model))
    gb_ref[...] = jax.nn.sigmoid(cols(_C_GA + d_model, d_model))


def _rope_tables(pos, head_w):
    half = head_w // ROPE_FRACTION // 2
    inv_freq = ROPE_THETA ** (-jnp.arange(half, dtype=F32) / half)
    ang = pos.astype(F32)[:, None] * inv_freq[None, :]
    cos, sin = jnp.cos(ang), jnp.sin(ang)
    n = pos.shape[0]
    rest = head_w - 2 * half
    c = jnp.concatenate([cos, cos, jnp.ones((n, rest), F32)], axis=1)
    s = jnp.concatenate([-sin, sin, jnp.zeros((n, rest), F32)], axis=1)
    reps = LANES // head_w
    return jnp.tile(c, (1, reps)), jnp.tile(s, (1, reps))


def _pack_w_in(w_in):
    d = w_in.shape[0]
    pad = jnp.zeros((d, LANES - IDX_DIM - IDX_HEADS), w_in.dtype)
    kw_end = _C_KW + IDX_DIM + IDX_HEADS
    return jnp.concatenate([w_in[:, :_C_KW], w_in[:, _C_KW:kw_end], pad, w_in[:, kw_end:]], axis=1).astype(BF16)


def _proj(x2d, pos, g_mix, w_packed, *, tm):
    n, d = x2d.shape
    p = pos.shape[0]
    tm = min(tm, p)
    assert n % tm == 0 and p % tm == 0
    c64, s64 = _rope_tables(pos, HEAD_DIM)
    c32, s32 = _rope_tables(pos, IDX_DIM)
    pblocks = p // tm
    row = lambda w: pl.BlockSpec((tm, w), lambda i: (i, 0))
    tab = pl.BlockSpec((tm, LANES), lambda i: (i % pblocks, 0))
    const = lambda shape: pl.BlockSpec(shape, lambda i: (0, 0))
    widths_f32 = [SB_W, SB_W, DSA_KV_W, DSA_KV_W, IDX_DIM]
    widths_b16 = [SB_W, SB_W, SB_W, DSA_Q_W, DSA_KV_W, DSA_KV_W, IDX_Q_W, IDX_DIM]
    out_shape = ([jax.ShapeDtypeStruct((n, w), F32) for w in widths_f32]
                 + [jax.ShapeDtypeStruct((n, w), BF16) for w in widths_b16]
                 + [jax.ShapeDtypeStruct((n, IDX_HEADS), F32),
                    jax.ShapeDtypeStruct((n, d), F32), jax.ShapeDtypeStruct((n, d), F32)])
    out_specs = [row(w) for w in widths_f32 + widths_b16] + [row(IDX_HEADS), row(d), row(d)]
    return pl.pallas_call(
        functools.partial(_proj_kernel, d_model=d),
        grid=(n // tm,),
        in_specs=[row(d), const((1, d)), const(w_packed.shape), tab, tab, tab, tab],
        out_specs=out_specs,
        out_shape=out_shape,
        compiler_params=pltpu.CompilerParams(dimension_semantics=("arbitrary",), vmem_limit_bytes=VMEM_LIMIT),
        name="proj",
    )(x2d, g_mix.reshape(1, d), w_packed, c64, s64, c32, s32)


def _sb_block(qm, k_tile, v_tile, acc, tri, mask):
    z = _dot_nt(qm, k_tile)
    l = -_softplus(z)
    if mask is not None:
        l = jnp.where(mask, l, 0.0)
    l_hi, l_lo = _split_bf16(l)
    cs = _dot(l_hi, tri) + _dot(l_lo, tri)
    e = z + cs + acc
    a = jnp.exp(e)
    if mask is not None:
        a = jnp.where(mask, a, 0.0)
    return _dot(a.astype(BF16), v_tile), acc + cs[:, 0:1]


def _sb_prompt_kernel(q_ref, k_ref, v_ref, o_ref, acc_ref, out_ref, *, tq):
    i = pl.program_id(1)
    n_pairs = SB_HEADS // 2
    tri = _tri(tq, lower=True)
    lane = _lane_iota((tq, LANES))
    diag_mask = _lane_iota((tq, tq)) < _row_iota((tq, tq))

    def qm(h):
        p, r = divmod(h, 2)
        qt = q_ref[0, :, p * LANES:(p + 1) * LANES]
        keep = (lane < HEAD_DIM) if r == 0 else (lane >= HEAD_DIM)
        return jnp.where(keep, qt, jnp.zeros_like(qt))

    def do_block(j, mask, first):
        start = pl.multiple_of(j * tq, tq)
        worst = None
        for h in range(SB_HEADS):
            p = h // 2
            k_tile = k_ref[0, pl.ds(start, tq), p * LANES:(p + 1) * LANES]
            v_tile = v_ref[0, pl.ds(start, tq), p * LANES:(p + 1) * LANES]
            acc = jnp.zeros((tq, 1), F32) if first else acc_ref[h]
            contrib, acc = _sb_block(qm(h), k_tile, v_tile, acc, tri, mask)
            acc_ref[h] = acc
            if first:
                out_ref[h] = contrib
            else:
                out_ref[h] += contrib
            worst = acc if worst is None else jnp.maximum(worst, acc)
        return (jnp.max(worst) > SB_DEAD_LOG).astype(jnp.int32)

    go0 = do_block(i, diag_mask, True)

    def cond(c):
        j, go = c
        return jnp.logical_and(j >= 0, go > 0)

    def body(c):
        j, _ = c
        return j - 1, do_block(j, None, False)

    lax.while_loop(cond, body, (i - 1, go0))

    for p in range(n_pairs):
        o_ref[0, :, p * LANES:(p + 1) * LANES] = jnp.where(
            lane < HEAD_DIM, out_ref[2 * p], out_ref[2 * p + 1]).astype(o_ref.dtype)


def _whole_seq_spec(s, w):
    return pl.BlockSpec((1, s, w), lambda b, i: (b, 0, 0), pipeline_mode=pl.Buffered(1))


def _sb_prompt(qa16, ka16, va16, *, tq=KEY_BLOCK):
    b, s, w = qa16.shape
    assert s % tq == 0 and w == SB_W
    return pl.pallas_call(
        functools.partial(_sb_prompt_kernel, tq=tq),
        grid=(b, s // tq),
        in_specs=[pl.BlockSpec((1, tq, w), lambda bi, i: (bi, i, 0)), _whole_seq_spec(s, w), _whole_seq_spec(s, w)],
        out_specs=pl.BlockSpec((1, tq, w), lambda bi, i: (bi, i, 0)),
        out_shape=jax.ShapeDtypeStruct((b, s, w), BF16),
        scratch_shapes=[pltpu.VMEM((SB_HEADS, tq, 1), F32), pltpu.VMEM((SB_HEADS, tq, LANES), F32)],
        compiler_params=pltpu.CompilerParams(dimension_semantics=("arbitrary", "arbitrary"),
                                             vmem_limit_bytes=VMEM_LIMIT),
        name="sb_prompt",
    )(qa16, ka16, va16)


def _topk_threshold(count_ge, topk, shape):
    def bit_body(it, u):
        bit = lax.shift_left(jnp.int32(1), 31 - it)
        cand = u | bit
        c = count_ge(cand ^ INT_MIN)
        return jnp.where(c >= topk, cand, u)

    u = lax.fori_loop(0, 32, bit_body, jnp.zeros(shape, jnp.int32))
    return u ^ INT_MIN


def _select(key, tau, need, run, tri_prefix):
    eq = key == tau
    ties = jnp.where(eq, 1.0, 0.0)
    pref = _dot(ties.astype(BF16), tri_prefix) + run
    sel = jnp.logical_or(key > tau, jnp.logical_and(eq, pref <= need))
    return sel, run + jnp.sum(ties, axis=-1, keepdims=True)


def _softmax_step(z, sel, m, l, acc, pv):
    zm = jnp.where(sel, z, NEG_BIG)
    m_new = jnp.maximum(m, jnp.max(zm, axis=-1, keepdims=True))
    alpha = jnp.exp(m - m_new)
    p = jnp.where(sel, jnp.exp(z - m_new), 0.0)
    l_new = alpha * l + jnp.sum(p, axis=-1, keepdims=True)
    acc_new = alpha * acc + pv(p.astype(BF16))
    return m_new, l_new, acc_new


BISECT_UNROLL = 4
ONES_ROWS = 16
TINY_DENOMINATOR = 1e-30


def _dsa_prompt_kernel(qbT_ref, qiT_ref, wiT_ref, kb_ref, vbT_ref, ki_ref, oT_ref,
                       keys_ref, qs_ref, qg_ref, m_ref, mx_ref, acc_ref, p_ref, kn2_ref, *, tq, topk, unroll):
    i = pl.program_id(1)
    tk = KEY_BLOCK
    rep = DSA_HEADS // DSA_KV_HEADS
    q_pos = i * tq + _lane_iota((tk, tq))
    key_off = _row_iota((tk, tq))
    idx_scale = IDX_DIM ** -0.5

    for h in range(IDX_HEADS):
        qs_ref[:, h * tq:(h + 1) * tq] = qiT_ref[0, h * IDX_DIM:(h + 1) * IDX_DIM, :]
    qg_ref[...] = jnp.zeros(qg_ref.shape, BF16)
    for h in range(DSA_HEADS):
        g, r = divmod(h, rep)
        half = g % 2
        qg_ref[g, half * HEAD_DIM:(half + 1) * HEAD_DIM, r * tq:(r + 1) * tq] = (
            qbT_ref[0, h * HEAD_DIM:(h + 1) * HEAD_DIM, :])

    n_trips = (i + unroll) // unroll

    def score_body(jg, carry):
        for k in range(unroll):
            j = jg * unroll + k
            start = pl.multiple_of(j * tk, tk)
            sc = _dot(ki_ref[0, pl.ds(start, tk), :], qs_ref[...])
            tot = jnp.zeros((tk, tq), F32)
            for h in range(IDX_HEADS):
                tot = tot + jnp.maximum(sc[:, h * tq:(h + 1) * tq] * idx_scale, 0.0) * wiT_ref[0, h:h + 1, :]
            causal = (j * tk + key_off) <= q_pos
            keys_ref[j] = jnp.where(causal, _sort_key(tot), INT_MIN)
        return carry

    lax.fori_loop(0, n_trips, score_body, 0)

    def count(pred):
        def body(jg, c):
            for k in range(unroll):
                c = c + jnp.where(pred(keys_ref[jg * unroll + k]), 1.0, 0.0)
            return c
        c = lax.fori_loop(0, n_trips, body, jnp.zeros((tk, tq), F32))
        return jnp.sum(c, axis=0, keepdims=True)

    tau = _topk_threshold(lambda v: count(lambda key: key >= v), float(topk), (1, tq))
    need = float(topk) - count(lambda key: key > tau)
    surplus_ties = jnp.max(count(lambda key: key >= tau)) > float(topk)

    tri_prefix = _tri(tk, lower=True)

    def selected(j, run, ordered_ties):
        key = keys_ref[j]
        if ordered_ties:
            eq = key == tau
            ties = jnp.where(eq, 1.0, 0.0)
            pref = _dot(tri_prefix, ties.astype(BF16)) + run
            sel = jnp.logical_or(key > tau, jnp.logical_and(eq, pref <= need))
            run = run + jnp.sum(ties, axis=0, keepdims=True)
        else:
            sel = key >= tau
        sel = jnp.logical_and(sel, (j * tk + key_off) <= q_pos)
        return jnp.concatenate([sel] * rep, axis=1), run

    def logits(j, g):
        start = pl.multiple_of(j * tk, tk)
        t = g // 2
        return _dot(kb_ref[0, pl.ds(start, tk), t * LANES:(t + 1) * LANES], qg_ref[g])

    def max_body(jg, run, ordered_ties):
        for k in range(unroll):
            j = jg * unroll + k
            sel_g, run = selected(j, run, ordered_ties)
            for g in range(DSA_KV_HEADS):
                mx_ref[g] = jnp.maximum(mx_ref[g], jnp.where(sel_g, logits(j, g), NEG_BIG))
        return run

    def attn_body(jg, run, ordered_ties):
        for k in range(unroll):
            j = jg * unroll + k
            sel_g, run = selected(j, run, ordered_ties)
            for g in range(DSA_KV_HEADS):
                p = jnp.where(sel_g, jnp.exp(logits(j, g) - m_ref[g]), 0.0)
                p_ref[k, g] = p.astype(BF16)
        for g in range(DSA_KV_HEADS):
            pv = _dot(vbT_ref[0, jg * unroll, g], p_ref[0, g])
            for k in range(1, unroll):
                pv = pv + _dot(vbT_ref[0, jg * unroll + k, g], p_ref[k, g])
            acc_ref[g] += pv
        return run

    run0 = jnp.zeros((1, tq), F32)

    def attend(ordered_ties):
        acc_ref[...] = jnp.zeros(acc_ref.shape, F32)
        lax.fori_loop(0, n_trips, functools.partial(attn_body, ordered_ties=ordered_ties), run0)

    @pl.when(i == 0)
    def _():
        lane = _lane_iota((tk, LANES))
        for g in range(DSA_KV_HEADS):
            t, half = divmod(g, 2)
            keep = (lane < HEAD_DIM) if half == 0 else (lane >= HEAD_DIM)

            def norm_body(j, best):
                start = pl.multiple_of(j * tk, tk)
                kf = kb_ref[0, pl.ds(start, tk), t * LANES:(t + 1) * LANES].astype(F32)
                return jnp.maximum(best, jnp.sum(jnp.where(keep, kf * kf, 0.0), axis=1, keepdims=True))

            best = lax.fori_loop(0, kb_ref.shape[1] // tk, norm_body, jnp.zeros((tk, 1), F32))
            kn2_ref[g] = jnp.max(best)

    for g in range(DSA_KV_HEADS):
        qn2 = []
        for r in range(rep):
            h = g * rep + r
            qf = qbT_ref[0, h * HEAD_DIM:(h + 1) * HEAD_DIM, :].astype(F32)
            qn2.append(jnp.sum(qf * qf, axis=0, keepdims=True))
        m_ref[g] = jnp.sqrt(jnp.concatenate(qn2, axis=1) * kn2_ref[g])

    lax.cond(surplus_ties, lambda: attend(True), lambda: attend(False))

    def smallest_denominator():
        lo = jnp.min(acc_ref[0, HEAD_DIM:HEAD_DIM + 1, :])
        for g in range(1, DSA_KV_HEADS):
            lo = jnp.minimum(lo, jnp.min(acc_ref[g, HEAD_DIM:HEAD_DIM + 1, :]))
        return lo

    @pl.when(jnp.logical_not(smallest_denominator() > TINY_DENOMINATOR))
    def _():
        mx_ref[...] = jnp.full(mx_ref.shape, NEG_BIG, F32)
        lax.fori_loop(0, n_trips, functools.partial(max_body, ordered_ties=True), run0)
        for g in range(DSA_KV_HEADS):
            m_ref[g] = jnp.max(mx_ref[g], axis=0, keepdims=True)
        attend(True)

    for h in range(DSA_HEADS):
        g, r = divmod(h, rep)
        o = (acc_ref[g, 0:HEAD_DIM, r * tq:(r + 1) * tq]
             / acc_ref[g, HEAD_DIM:HEAD_DIM + 1, r * tq:(r + 1) * tq])
        oT_ref[0, h * HEAD_DIM:(h + 1) * HEAD_DIM, :] = o.astype(oT_ref.dtype)


def _dsa_prompt(qb16, qi16, wi, kb16, vb16, ki16, *, tq=KEY_BLOCK):
    b, s, _ = qb16.shape
    tk = KEY_BLOCK
    assert s % tq == 0 and tq == tk
    n_kb = s // tk
    topk = min(TOPK_MAX, s // 4)
    unroll = BISECT_UNROLL if n_kb % BISECT_UNROLL == 0 else 1
    rep = DSA_HEADS // DSA_KV_HEADS
    tr = lambda a: jnp.transpose(a, (0, 2, 1))
    vbT = jnp.transpose(vb16.reshape(b, n_kb, tk, DSA_KV_HEADS, HEAD_DIM), (0, 1, 3, 4, 2))
    vbT = jnp.concatenate([vbT, jnp.ones((b, n_kb, DSA_KV_HEADS, ONES_ROWS, tk), BF16)], axis=3)
    v_rows = HEAD_DIM + ONES_ROWS
    qcol = lambda w: pl.BlockSpec((1, w, tq), lambda bi, i: (bi, 0, i))
    whole = lambda shape: pl.BlockSpec((1,) + shape, lambda bi, i: (bi,) + (0,) * len(shape),
                                       pipeline_mode=pl.Buffered(1))
    oT = pl.pallas_call(
        functools.partial(_dsa_prompt_kernel, tq=tq, topk=topk, unroll=unroll),
        grid=(b, s // tq),
        in_specs=[qcol(DSA_Q_W), qcol(IDX_Q_W), qcol(IDX_HEADS),
                  whole((s, DSA_KV_W)), whole((n_kb, DSA_KV_HEADS, v_rows, tk)), whole((s, IDX_DIM))],
        out_specs=qcol(DSA_Q_W),
        out_shape=jax.ShapeDtypeStruct((b, DSA_Q_W, s), BF16),
        scratch_shapes=[
            pltpu.VMEM((n_kb, tk, tq), jnp.int32),
            pltpu.VMEM((IDX_DIM, IDX_HEADS * tq), BF16),
            pltpu.VMEM((DSA_KV_HEADS, LANES, rep * tq), BF16),
            pltpu.VMEM((DSA_KV_HEADS, 1, rep * tq), F32),
            pltpu.VMEM((DSA_KV_HEADS, tk, rep * tq), F32),
            pltpu.VMEM((DSA_KV_HEADS, v_rows, rep * tq), F32),
            pltpu.VMEM((unroll, DSA_KV_HEADS, tk, rep * tq), BF16),
            pltpu.SMEM((DSA_KV_HEADS,), F32),
        ],
        compiler_params=pltpu.CompilerParams(dimension_semantics=("arbitrary", "arbitrary"),
                                             vmem_limit_bytes=VMEM_LIMIT),
        name="dsa_prompt",
    )(tr(qb16), tr(qi16), tr(wi), kb16, vbT, ki16)
    return tr(oT)


def _pages_key_minor(cache):
    c = cache[0]
    n_phys, page = c.shape[:2]
    perm = (0, 2, 3, 1) if c.ndim == 4 else (0, 2, 1)
    return jnp.transpose(c, perm).reshape(n_phys, -1, page)


def _head_rows(q, n_groups, group_of_head):
    bd, t, heads, hd = q.shape
    onehot = np.zeros((heads, n_groups), np.float32)
    for h in range(heads):
        onehot[h, group_of_head(h)] = 1.0
    qt = jnp.transpose(q, (0, 2, 1, 3))
    out = qt[:, :, :, None, :] * jnp.asarray(onehot, q.dtype)[None, :, None, :, None]
    return out.reshape(bd, heads * t, n_groups * hd)


def _sb_sample_kernel(pt_ref, q_ref, kn_ref, vn_ref, kc_hbm, vc_hbm, o_ref,
                      kbuf, vbuf, sem, acc_ref, out_ref, kpad_ref, vpad_ref, *, t, n_pages):
    b = pl.program_id(0)
    rows = SB_HEADS * t
    tri = _tri(KEY_BLOCK, lower=True)

    def page_copies(p, slot):
        page = pt_ref[b, n_pages - 1 - p]
        return (pltpu.make_async_copy(kc_hbm.at[page], kbuf.at[slot], sem.at[0, slot]),
                pltpu.make_async_copy(vc_hbm.at[page], vbuf.at[slot], sem.at[1, slot]))

    def weigh(z, acc, mask):
        l = -_softplus(z)
        if mask is not None:
            l = jnp.where(mask, l, 0.0)
        l_hi, l_lo = _split_bf16(l)
        cs = _dot(l_hi, tri) + _dot(l_lo, tri)
        a = jnp.exp(z + cs + acc)
        if mask is not None:
            a = jnp.where(mask, a, 0.0)
        return a.astype(BF16), acc + cs[:, 0:1]

    def alive(acc):
        return (jnp.max(acc) > SB_DEAD_LOG).astype(jnp.int32)

    for cp in page_copies(0, 0):
        cp.start()

    kpad_ref[...] = jnp.zeros(kpad_ref.shape, BF16)
    vpad_ref[...] = jnp.zeros(vpad_ref.shape, BF16)
    kpad_ref[0:t, :] = kn_ref[0]
    vpad_ref[0:t, :] = vn_ref[0]
    q_time = _row_iota((rows, KEY_BLOCK)) & (t - 1)
    mask = _lane_iota((rows, KEY_BLOCK)) < q_time
    a, acc = weigh(_dot_nt(q_ref[0], kpad_ref[...]), jnp.zeros((rows, 1), F32), mask)
    out_ref[...] = _dot(a, vpad_ref[...])
    acc_ref[...] = acc

    def cond(c):
        p, go = c
        return jnp.logical_and(p < n_pages, go > 0)

    def body(c):
        p, _ = c
        slot = lax.rem(p, 2)
        for cp in page_copies(p, slot):
            cp.wait()

        @pl.when(p + 1 < n_pages)
        def _():
            for cp in page_copies(p + 1, 1 - slot):
                cp.start()

        a, acc = weigh(_dot(q_ref[0], kbuf[slot].astype(BF16)), acc_ref[...], None)
        out_ref[...] += _dot_nt(a, vbuf[slot].astype(BF16))
        acc_ref[...] = acc
        return p + 1, alive(acc)

    p_end, _ = lax.while_loop(cond, body, (jnp.int32(0), alive(acc)))

    @pl.when(p_end < n_pages)
    def _():
        for cp in page_copies(p_end, lax.rem(p_end, 2)):
            cp.wait()

    lane = _lane_iota((t, SB_W))
    res = jnp.zeros((t, SB_W), F32)
    for h in range(SB_HEADS):
        res = res + jnp.where(lane // HEAD_DIM == h, out_ref[h * t:(h + 1) * t, :], 0.0)
    o_ref[0] = res.astype(o_ref.dtype)


def _sb_sample(qa16, ka16, va16, cache_kt, cache_vt, page_table):
    bd, t, w = qa16.shape
    n_pages = page_table.shape[1]
    page = cache_kt.shape[2]
    assert page == KEY_BLOCK and t & (t - 1) == 0 and t % 8 == 0 and n_pages >= 1
    q_rows = _head_rows(qa16.reshape(bd, t, SB_HEADS, HEAD_DIM), SB_HEADS, lambda h: h)
    rows = SB_HEADS * t
    per_b = lambda r, c: pl.BlockSpec((1, r, c), lambda b, pt: (b, 0, 0))
    return pl.pallas_call(
        functools.partial(_sb_sample_kernel, t=t, n_pages=n_pages),
        grid_spec=pltpu.PrefetchScalarGridSpec(
            num_scalar_prefetch=1,
            grid=(bd,),
            in_specs=[per_b(rows, w), per_b(t, w), per_b(t, w),
                      pl.BlockSpec(memory_space=pl.ANY), pl.BlockSpec(memory_space=pl.ANY)],
            out_specs=per_b(t, w),
            scratch_shapes=[pltpu.VMEM((2, w, page), F32), pltpu.VMEM((2, w, page), F32),
                            pltpu.SemaphoreType.DMA((2, 2)),
                            pltpu.VMEM((rows, 1), F32), pltpu.VMEM((rows, w), F32),
                            pltpu.VMEM((KEY_BLOCK, w), BF16), pltpu.VMEM((KEY_BLOCK, w), BF16)],
        ),
        out_shape=jax.ShapeDtypeStruct((bd, t, w), BF16),
        compiler_params=pltpu.CompilerParams(dimension_semantics=("arbitrary",), vmem_limit_bytes=VMEM_LIMIT),
        name="sb_sample",
    )(page_table, q_rows, ka16, va16, cache_kt, cache_vt)


PAGES_PER_STEP = 16


def _dsa_scores_kernel(pt_ref, q_ref, w_ref, kn_ref, *rest, t, n_pages, topk, pp):
    kc_refs = rest[:pp]
    keys_ref, keysn_ref, tau_ref, need_ref, kpad_ref = rest[pp:]
    step = pl.program_id(1)
    idx_scale = IDX_DIM ** -0.5

    def keys_of(sc):
        rel = jnp.maximum(sc * idx_scale, 0.0) * w_ref[0]
        tot = jnp.zeros((t, sc.shape[1]), F32)
        for h in range(IDX_HEADS):
            tot = tot + rel[h * t:(h + 1) * t, :]
        return _sort_key(tot)

    kt = jnp.concatenate([r[0].astype(BF16) for r in kc_refs], axis=1)
    key = keys_of(_dot(q_ref[0], kt))
    for k in range(pp):
        keys_ref[0, step * pp + k] = key[:, k * KEY_BLOCK:(k + 1) * KEY_BLOCK]

    @pl.when(step == n_pages // pp - 1)
    def _():
        kpad_ref[...] = jnp.zeros(kpad_ref.shape, BF16)
        kpad_ref[0:t, :] = kn_ref[0]
        causal = _lane_iota((t, KEY_BLOCK)) <= _row_iota((t, KEY_BLOCK))
        keysn_ref[0] = jnp.where(causal, keys_of(_dot_nt(q_ref[0], kpad_ref[...])), INT_MIN)

        def count(pred):
            def body(j, c):
                return c + jnp.where(pred(keys_ref[0, j]), 1.0, 0.0)
            c = lax.fori_loop(0, n_pages, body, jnp.where(pred(keysn_ref[0]), 1.0, 0.0))
            return jnp.sum(c, axis=-1, keepdims=True)

        tau = _topk_threshold(lambda v: count(lambda key: key >= v), float(topk), (t, 1))
        tau_ref[0] = tau
        need_ref[0] = float(topk) - count(lambda key: key > tau)


def _dsa_attn_kernel(pt_ref, q_ref, kn_ref, vn_ref, keys_ref, keysn_ref, tau_ref, need_ref, *rest, t, n_pages, pp):
    kc_refs, vc_refs = rest[:pp], rest[pp:2 * pp]
    o_ref, m_ref, l_ref, acc_ref, run_ref, kpad_ref, vpad_ref = rest[2 * pp:]
    step = pl.program_id(1)
    tri_prefix = _tri(KEY_BLOCK, lower=False)
    tau, need = tau_ref[0], need_ref[0]

    @pl.when(step == 0)
    def _():
        m_ref[...] = jnp.full(m_ref.shape, NEG_BIG, F32)
        l_ref[...] = jnp.zeros(l_ref.shape, F32)
        acc_ref[...] = jnp.zeros(acc_ref.shape, F32)
        run_ref[...] = jnp.zeros(run_ref.shape, F32)

    def update(z, sel, pv):
        sel_rows = jnp.concatenate([sel] * DSA_HEADS, axis=0)
        m_ref[...], l_ref[...], acc_ref[...] = _softmax_step(z, sel_rows, m_ref[...], l_ref[...], acc_ref[...], pv)

    sels = []
    run = run_ref[...]
    for k in range(pp):
        sel, run = _select(keys_ref[0, k], tau, need, run, tri_prefix)
        sels.append(sel)
    kt = jnp.concatenate([r[0].astype(BF16) for r in kc_refs], axis=1)

    def pv_pages(p16):
        tot = _dot_nt(p16[:, 0:KEY_BLOCK], vc_refs[0][0].astype(BF16))
        for k in range(1, pp):
            tot = tot + _dot_nt(p16[:, k * KEY_BLOCK:(k + 1) * KEY_BLOCK], vc_refs[k][0].astype(BF16))
        return tot

    update(_dot(q_ref[0], kt), jnp.concatenate(sels, axis=1), pv_pages)
    run_ref[...] = run

    @pl.when(step == n_pages // pp - 1)
    def _():
        kpad_ref[...] = jnp.zeros(kpad_ref.shape, BF16)
        vpad_ref[...] = jnp.zeros(vpad_ref.shape, BF16)
        kpad_ref[0:t, :] = kn_ref[0]
        vpad_ref[0:t, :] = vn_ref[0]
        sel, _ = _select(keysn_ref[0], tau, need, run, tri_prefix)
        causal = _lane_iota((t, KEY_BLOCK)) <= _row_iota((t, KEY_BLOCK))
        update(_dot_nt(q_ref[0], kpad_ref[...]), jnp.logical_and(sel, causal), lambda p16: _dot(p16, vpad_ref[...]))
        o = acc_ref[...] / l_ref[...]
        lane = _lane_iota((t, DSA_KV_W))
        for h in range(DSA_HEADS):
            g = h // (DSA_HEADS // DSA_KV_HEADS)
            o_ref[0, h] = jnp.where(lane // HEAD_DIM == g, o[h * t:(h + 1) * t, :], 0.0)


def _dsa_sample(qb16, kb16, vb16, qi16, ki16, wi, cache_kt, cache_vt, cache_it, page_table):
    bd, t, _ = qb16.shape
    n_pages = page_table.shape[1]
    page = cache_kt.shape[2]
    pp = min(PAGES_PER_STEP, n_pages)
    assert page == KEY_BLOCK and t % 8 == 0 and n_pages % pp == 0
    n_steps = n_pages // pp
    topk = min(TOPK_MAX, (n_pages * page + t) // 4)
    rep = DSA_HEADS // DSA_KV_HEADS

    qi_rows = jnp.transpose(qi16.reshape(bd, t, IDX_HEADS, IDX_DIM), (0, 2, 1, 3)).reshape(bd, IDX_HEADS * t, IDX_DIM)
    wi_rows = jnp.transpose(wi, (0, 2, 1)).reshape(bd, IDX_HEADS * t, 1)
    qb_rows = _head_rows(qb16.reshape(bd, t, DSA_HEADS, HEAD_DIM), DSA_KV_HEADS, lambda h: h // rep)

    per_b = lambda r, c: pl.BlockSpec((1, r, c), lambda b, s, pt: (b, 0, 0))
    page_spec = lambda w, k: pl.BlockSpec((1, w, page), lambda b, s, pt: (pt[b, s * pp + k], 0, 0))
    cparams = pltpu.CompilerParams(dimension_semantics=("arbitrary", "arbitrary"), vmem_limit_bytes=VMEM_LIMIT)

    keys, keys_new, tau, need = pl.pallas_call(
        functools.partial(_dsa_scores_kernel, t=t, n_pages=n_pages, topk=topk, pp=pp),
        grid_spec=pltpu.PrefetchScalarGridSpec(
            num_scalar_prefetch=1,
            grid=(bd, n_steps),
            in_specs=[per_b(IDX_HEADS * t, IDX_DIM), per_b(IDX_HEADS * t, 1), per_b(t, IDX_DIM)]
                     + [page_spec(IDX_DIM, k) for k in range(pp)],
            out_specs=[pl.BlockSpec((1, n_pages, t, KEY_BLOCK), lambda b, s, pt: (b, 0, 0, 0)),
                       per_b(t, KEY_BLOCK), per_b(t, 1), per_b(t, 1)],
            scratch_shapes=[pltpu.VMEM((KEY_BLOCK, IDX_DIM), BF16)],
        ),
        out_shape=[jax.ShapeDtypeStruct((bd, n_pages, t, KEY_BLOCK), jnp.int32),
                   jax.ShapeDtypeStruct((bd, t, KEY_BLOCK), jnp.int32),
                   jax.ShapeDtypeStruct((bd, t, 1), jnp.int32), jax.ShapeDtypeStruct((bd, t, 1), F32)],
        compiler_params=cparams,
        name="dsa_scores",
    )(page_table, qi_rows, wi_rows, ki16, *([cache_it] * pp))

    rows = DSA_HEADS * t
    o_rows = pl.pallas_call(
        functools.partial(_dsa_attn_kernel, t=t, n_pages=n_pages, pp=pp),
        grid_spec=pltpu.PrefetchScalarGridSpec(
            num_scalar_prefetch=1,
            grid=(bd, n_steps),
            in_specs=[per_b(rows, DSA_KV_W), per_b(t, DSA_KV_W), per_b(t, DSA_KV_W),
                      pl.BlockSpec((1, pp, t, KEY_BLOCK), lambda b, s, pt: (b, s, 0, 0)),
                      per_b(t, KEY_BLOCK), per_b(t, 1), per_b(t, 1)]
                     + [page_spec(DSA_KV_W, k) for k in range(pp)] * 2,
            out_specs=pl.BlockSpec((1, DSA_HEADS, t, DSA_KV_W), lambda b, s, pt: (b, 0, 0, 0)),
            scratch_shapes=[pltpu.VMEM((rows, 1), F32), pltpu.VMEM((rows, 1), F32), pltpu.VMEM((rows, DSA_KV_W), F32),
                            pltpu.VMEM((t, 1), F32),
                            pltpu.VMEM((KEY_BLOCK, DSA_KV_W), BF16), pltpu.VMEM((KEY_BLOCK, DSA_KV_W), BF16)],
        ),
        out_shape=jax.ShapeDtypeStruct((bd, DSA_HEADS, t, DSA_KV_W), F32),
        compiler_params=cparams,
        name="dsa_attn",
    )(page_table, qb_rows, kb16, vb16, keys, keys_new, tau, need, *([cache_kt] * pp), *([cache_vt] * pp))
    o = o_rows.reshape(bd, DSA_HEADS, t, DSA_KV_HEADS, HEAD_DIM).sum(axis=3)
    return jnp.transpose(o, (0, 2, 1, 3)).reshape(bd, t, DSA_Q_W)


def _out_kernel(x_ref, oa_ref, ob_ref, ga_ref, gb_ref, wpa_ref, wpb_ref, wo_ref, gf_ref, wg_ref, wu_ref, wd_ref,
                gfin_ref, y_ref):
    m = ga_ref[...] * _dot(oa_ref[...], wpa_ref[...]) + gb_ref[...] * _dot(ob_ref[...], wpb_ref[...])
    x = x_ref[...] + _dot(m.astype(BF16), wo_ref[...])
    h = x * lax.rsqrt(jnp.mean(x * x, axis=-1, keepdims=True) + RMS_EPS) * gf_ref[...]
    h16 = h.astype(BF16)
    gate = _dot(h16, wg_ref[...])
    up = _dot(h16, wu_ref[...])
    f = _dot((gate * jax.nn.sigmoid(gate) * up).astype(BF16), wd_ref[...])
    x = x + f
    y_ref[...] = x * lax.rsqrt(jnp.mean(x * x, axis=-1, keepdims=True) + RMS_EPS) * gfin_ref[...]


def _out(x2d, oa16, ob16, ga, gb, w_pa, w_pb, w_o, g_ffn, w_gate, w_up, w_down, g_final, *, tm):
    n, d = x2d.shape
    tm = min(tm, n)
    assert n % tm == 0
    row = lambda w: pl.BlockSpec((tm, w), lambda i: (i, 0))
    const = lambda a: pl.BlockSpec(a.shape, lambda i: (0, 0), pipeline_mode=pl.Buffered(1))
    weights = [w_pa, w_pb, w_o]
    ffn = [w_gate, w_up, w_down]
    gf, gfin = g_ffn.reshape(1, d), g_final.reshape(1, d)
    args = [x2d, oa16, ob16, ga, gb, *weights, gf, *ffn, gfin]
    in_specs = [row(d), row(oa16.shape[1]), row(ob16.shape[1]), row(d), row(d),
                *[const(w) for w in weights], const(gf), *[const(w) for w in ffn], const(gfin)]
    return pl.pallas_call(
        _out_kernel,
        grid=(n // tm,),
        in_specs=in_specs,
        out_specs=row(d),
        out_shape=jax.ShapeDtypeStruct((n, d), F32),
        compiler_params=pltpu.CompilerParams(dimension_semantics=("arbitrary",), vmem_limit_bytes=VMEM_LIMIT),
        name="out",
    )(*args)


def kernel(x_prompt, x_sample, cache_sb_k, cache_sb_v, cache_dsa_k, cache_dsa_v, cache_idx_k, page_table,
           g_mix, w_in, w_pa, w_pb, w_o, g_ffn, w_gate, w_up, w_down, g_final):
    b, s, d = x_prompt.shape
    bd, t, _ = x_sample.shape
    depth, n_phys, page = cache_sb_k.shape[:3]
    assert depth == 1, "single-layer step"
    past = page_table.shape[1] * page
    n_pg = s // page

    w_packed = _pack_w_in(w_in[0])
    pos_p = jnp.arange(s)
    pos_s = past + jnp.arange(bd * t) % t
    xp2 = x_prompt.reshape(b * s, d)
    xs2 = x_sample.reshape(bd * t, d)

    (ka, va, kb, vb, ki, qa16, ka16, va16, qb16, kb16, vb16, qi16, ki16, wi, ga, gb) = _proj(
        xp2, pos_p, g_mix[0], w_packed, tm=256)
    (ska, sva, skb, svb, ski, sqa16, ska16, sva16, sqb16, skb16, svb16, sqi16, ski16, swi, sga, sgb) = _proj(
        xs2, pos_s, g_mix[0], w_packed, tm=256)

    seq = lambda a: a.reshape(b, s, a.shape[-1])
    dec = lambda a: a.reshape(bd, t, a.shape[-1])

    oa = _sb_prompt(seq(qa16), seq(ka16), seq(va16))
    ob = _dsa_prompt(seq(qb16), seq(qi16), seq(wi), seq(kb16), seq(vb16), seq(ki16))
    b16 = lambda w: w[0].astype(BF16)
    out_w = (b16(w_pa), b16(w_pb), b16(w_o), g_ffn[0], b16(w_gate), b16(w_up), b16(w_down), g_final)
    y_prompt = _out(xp2, oa.reshape(b * s, SB_W), ob.reshape(b * s, DSA_Q_W), ga, gb, *out_w, tm=256).reshape(b, s, d)

    soa = _sb_sample(dec(sqa16), dec(ska16), dec(sva16), _pages_key_minor(cache_sb_k), _pages_key_minor(cache_sb_v),
                     page_table)
    sob = _dsa_sample(dec(sqb16), dec(skb16), dec(svb16), dec(sqi16), dec(ski16), dec(swi),
                      _pages_key_minor(cache_dsa_k), _pages_key_minor(cache_dsa_v), _pages_key_minor(cache_idx_k),
                      page_table)
    y_sample = _out(xs2, soa.reshape(bd * t, SB_W), sob.reshape(bd * t, DSA_Q_W).astype(BF16), sga, sgb, *out_w,
                    tm=256).reshape(bd, t, d)

    return (y_prompt, y_sample,
            ka.reshape(1, b, n_pg, page, SB_HEADS, HEAD_DIM), va.reshape(1, b, n_pg, page, SB_HEADS, HEAD_DIM),
            kb.reshape(1, b, n_pg, page, DSA_KV_HEADS, HEAD_DIM), vb.reshape(1, b, n_pg, page, DSA_KV_HEADS, HEAD_DIM),
            ki.reshape(1, b, n_pg, page, IDX_DIM),
            ska.reshape(1, bd, t, SB_HEADS, HEAD_DIM), sva.reshape(1, bd, t, SB_HEADS, HEAD_DIM),
            skb.reshape(1, bd, t, DSA_KV_HEADS, HEAD_DIM), svb.reshape(1, bd, t, DSA_KV_HEADS, HEAD_DIM),
            ski.reshape(1, bd, t, IDX_DIM))
```

```python
import functools

import jax
import jax.numpy as jnp
import numpy as np
from jax import lax
from jax.experimental import pallas as pl
from jax.experimental.pallas import tpu as pltpu

HEAD_DIM = 64
SB_HEADS = 8
DSA_HEADS = 8
DSA_KV_HEADS = 4
IDX_HEADS = 8
IDX_DIM = 32
TOPK_MAX = 256
ROPE_THETA = 500000.0
ROPE_FRACTION = 4
RMS_EPS = 1e-6

SB_W = SB_HEADS * HEAD_DIM
DSA_Q_W = DSA_HEADS * HEAD_DIM
DSA_KV_W = DSA_KV_HEADS * HEAD_DIM
IDX_Q_W = IDX_HEADS * IDX_DIM

LANES = 128
KEY_BLOCK = 128
INT_MIN = -(2 ** 31)
SB_DEAD_LOG = -104.0
NEG_BIG = -1e30
VMEM_LIMIT = 56 * 1024 * 1024

F32 = jnp.float32
BF16 = jnp.bfloat16

_NT = (((1,), (1,)), ((), ()))


def _dot(a, b):
    return jnp.dot(a, b, preferred_element_type=F32)


def _dot_nt(a, b):
    return lax.dot_general(a, b, _NT, preferred_element_type=F32)


def _lane_iota(shape):
    return lax.broadcasted_iota(jnp.int32, shape, len(shape) - 1)


def _row_iota(shape):
    return lax.broadcasted_iota(jnp.int32, shape, len(shape) - 2)


def _softplus(z):
    return jnp.maximum(z, 0.0) + jnp.log1p(jnp.exp(-jnp.abs(z)))


def _split_bf16(x):
    hi = x.astype(BF16)
    lo = (x - hi.astype(F32)).astype(BF16)
    return hi, lo


def _tri(n, lower):
    r = lax.broadcasted_iota(jnp.int32, (n, n), 0)
    c = lax.broadcasted_iota(jnp.int32, (n, n), 1)
    m = (r >= c) if lower else (r <= c)
    return jnp.where(m, 1.0, 0.0).astype(BF16)


def _sort_key(score):
    score = jnp.where(score == 0.0, 0.0, score)
    bits = pltpu.bitcast(score, jnp.int32)
    return bits ^ ((bits >> 31) & 0x7FFFFFFF)


_C_QA = 0
_C_KA = _C_QA + SB_W
_C_VA = _C_KA + SB_W
_C_QB = _C_VA + SB_W
_C_KB = _C_QB + DSA_Q_W
_C_VB = _C_KB + DSA_KV_W
_C_QI = _C_VB + DSA_KV_W
_C_KW = _C_QI + IDX_Q_W
_C_GA = _C_KW + LANES


def _rope(x, c, s, head_w):
    half = head_w // ROPE_FRACTION // 2
    first = (_lane_iota(c.shape) & (head_w - 1)) < half
    tiles = []
    for k in range(x.shape[-1] // LANES):
        xt = x[:, k * LANES:(k + 1) * LANES]
        rot = jnp.where(first, pltpu.roll(xt, LANES - half, 1), pltpu.roll(xt, half, 1))
        tiles.append(xt * c + rot * s)
    return tiles[0] if len(tiles) == 1 else jnp.concatenate(tiles, axis=1)


def _proj_kernel(x_ref, g_ref, w_ref, c64_ref, s64_ref, c32_ref, s32_ref,
                 ka_ref, va_ref, kb_ref, vb_ref, ki_ref,
                 qa16_ref, ka16_ref, va16_ref, qb16_ref, kb16_ref, vb16_ref, qi16_ref, ki16_ref,
                 wi_ref, ga_ref, gb_ref, *, d_model):
    x = x_ref[...]
    y = x * lax.rsqrt(jnp.mean(x * x, axis=-1, keepdims=True) + RMS_EPS)
    h = (y * g_ref[...]).astype(BF16)
    scale = HEAD_DIM ** -0.5

    def cols(lo, width):
        return _dot(h, w_ref[:, lo:lo + width])

    qa = cols(_C_QA, SB_W)
    qa16_ref[...] = (qa * scale).astype(BF16)
    ka = cols(_C_KA, SB_W)
    ka_ref[...] = ka
    ka16_ref[...] = ka.astype(BF16)
    va = cols(_C_VA, SB_W)
    va_ref[...] = va
    va16_ref[...] = va.astype(BF16)

    c64, s64 = c64_ref[...], s64_ref[...]
    qb = _rope(cols(_C_QB, DSA_Q_W), c64, s64, HEAD_DIM)
    qb16_ref[...] = (qb * scale).astype(BF16)
    kb = _rope(cols(_C_KB, DSA_KV_W), c64, s64, HEAD_DIM)
    kb_ref[...] = kb
    kb16_ref[...] = kb.astype(BF16)
    vb = cols(_C_VB, DSA_KV_W)
    vb_ref[...] = vb
    vb16_ref[...] = vb.astype(BF16)

    c32, s32 = c32_ref[...], s32_ref[...]
    qi = _rope(cols(_C_QI, IDX_Q_W), c32, s32, IDX_DIM)
    qi16_ref[...] = qi.astype(BF16)
    kw = cols(_C_KW, LANES)
    ki = _rope(kw, c32, s32, IDX_DIM)[:, :IDX_DIM]
    ki_ref[...] = ki
    ki16_ref[...] = ki.astype(BF16)
    wi_ref[...] = kw[:, IDX_DIM:IDX_DIM + IDX_HEADS] * (IDX_HEADS ** -0.5)

    ga_ref[...] = jax.nn.sigmoid(cols(_C_GA, d_model))
    gb_ref[...] = jax.nn.sigmoid(cols(_C_GA + d_model, d_model))


def _rope_tables(pos, head_w):
    half = head_w // ROPE_FRACTION // 2
    inv_freq = ROPE_THETA ** (-jnp.arange(half, dtype=F32) / half)
    ang = pos.astype(F32)[:, None] * inv_freq[None, :]
    cos, sin = jnp.cos(ang), jnp.sin(ang)
    n = pos.shape[0]
    rest = head_w - 2 * half
    c = jnp.concatenate([cos, cos, jnp.ones((n, rest), F32)], axis=1)
    s = jnp.concatenate([-sin, sin, jnp.zeros((n, rest), F32)], axis=1)
    reps = LANES // head_w
    return jnp.tile(c, (1, reps)), jnp.tile(s, (1, reps))


def _pack_w_in(w_in):
    d = w_in.shape[0]
    pad = jnp.zeros((d, LANES - IDX_DIM - IDX_HEADS), w_in.dtype)
    kw_end = _C_KW + IDX_DIM + IDX_HEADS
    return jnp.concatenate([w_in[:, :_C_KW], w_in[:, _C_KW:kw_end], pad, w_in[:, kw_end:]], axis=1).astype(BF16)


def _proj(x2d, pos, g_mix, w_packed, *, tm):
    n, d = x2d.shape
    p = pos.shape[0]
    tm = min(tm, p)
    assert n % tm == 0 and p % tm == 0
    c64, s64 = _rope_tables(pos, HEAD_DIM)
    c32, s32 = _rope_tables(pos, IDX_DIM)
    pblocks = p // tm
    row = lambda w: pl.BlockSpec((tm, w), lambda i: (i, 0))
    tab = pl.BlockSpec((tm, LANES), lambda i: (i % pblocks, 0))
    const = lambda shape: pl.BlockSpec(shape, lambda i: (0, 0))
    widths_f32 = [SB_W, SB_W, DSA_KV_W, DSA_KV_W, IDX_DIM]
    widths_b16 = [SB_W, SB_W, SB_W, DSA_Q_W, DSA_KV_W, DSA_KV_W, IDX_Q_W, IDX_DIM]
    out_shape = ([jax.ShapeDtypeStruct((n, w), F32) for w in widths_f32]
                 + [jax.ShapeDtypeStruct((n, w), BF16) for w in widths_b16]
                 + [jax.ShapeDtypeStruct((n, IDX_HEADS), F32),
                    jax.ShapeDtypeStruct((n, d), F32), jax.ShapeDtypeStruct((n, d), F32)])
    out_specs = [row(w) for w in widths_f32 + widths_b16] + [row(IDX_HEADS), row(d), row(d)]
    return pl.pallas_call(
        functools.partial(_proj_kernel, d_model=d),
        grid=(n // tm,),
        in_specs=[row(d), const((1, d)), const(w_packed.shape), tab, tab, tab, tab],
        out_specs=out_specs,
        out_shape=out_shape,
        compiler_params=pltpu.CompilerParams(dimension_semantics=("arbitrary",), vmem_limit_bytes=VMEM_LIMIT),
        name="proj",
    )(x2d, g_mix.reshape(1, d), w_packed, c64, s64, c32, s32)


def _sb_prompt_kernel(q_ref, k_ref, v_ref, o_ref, acc_ref, out_ref, qm_ref, z_ref, lhi_ref, llo_ref, a_ref, *, tq):
    i = pl.program_id(1)
    n_pairs = SB_HEADS // 2
    tri = _tri(tq, lower=True)
    lane = _lane_iota((tq, LANES))
    diag_mask = _lane_iota((tq, tq)) < _row_iota((tq, tq))

    for h in range(SB_HEADS):
        p, r = divmod(h, 2)
        qt = q_ref[0, :, p * LANES:(p + 1) * LANES]
        keep = (lane < HEAD_DIM) if r == 0 else (lane >= HEAD_DIM)
        qm_ref[h] = jnp.where(keep, qt, jnp.zeros_like(qt))

    def do_block(j, mask, first):
        start = pl.multiple_of(j * tq, tq)
        tile = lambda ref, h: ref[0, pl.ds(start, tq), (h // 2) * LANES:(h // 2 + 1) * LANES]
        for h in range(SB_HEADS):
            z = _dot_nt(qm_ref[h], tile(k_ref, h))
            l = -_softplus(z)
            if mask is not None:
                l = jnp.where(mask, l, 0.0)
            z_ref[h] = z
            lhi_ref[h], llo_ref[h] = _split_bf16(l)
        worst = None
        for h in range(SB_HEADS):
            cs = _dot(lhi_ref[h], tri) + _dot(llo_ref[h], tri)
            acc = jnp.zeros((tq, 1), F32) if first else acc_ref[h]
            a = jnp.exp(z_ref[h] + cs + acc)
            if mask is not None:
                a = jnp.where(mask, a, 0.0)
            a_ref[h] = a.astype(BF16)
            acc = acc + cs[:, 0:1]
            acc_ref[h] = acc
            worst = acc if worst is None else jnp.maximum(worst, acc)
        for h in range(SB_HEADS):
            contrib = _dot(a_ref[h], tile(v_ref, h))
            if first:
                out_ref[h] = contrib
            else:
                out_ref[h] += contrib
        return (jnp.max(worst) > SB_DEAD_LOG).astype(jnp.int32)

    go0 = do_block(i, diag_mask, True)

    def cond(c):
        j, go = c
        return jnp.logical_and(j >= 0, go > 0)

    def body(c):
        j, _ = c
        return j - 1, do_block(j, None, False)

    lax.while_loop(cond, body, (i - 1, go0))

    for p in range(n_pairs):
        o_ref[0, :, p * LANES:(p + 1) * LANES] = jnp.where(
            lane < HEAD_DIM, out_ref[2 * p], out_ref[2 * p + 1]).astype(o_ref.dtype)


def _whole_seq_spec(s, w):
    return pl.BlockSpec((1, s, w), lambda b, i: (b, 0, 0), pipeline_mode=pl.Buffered(1))


def _sb_prompt(qa16, ka16, va16, *, tq=KEY_BLOCK):
    b, s, w = qa16.shape
    assert s % tq == 0 and w == SB_W
    return pl.pallas_call(
        functools.partial(_sb_prompt_kernel, tq=tq),
        grid=(b, s // tq),
        in_specs=[pl.BlockSpec((1, tq, w), lambda bi, i: (bi, i, 0)), _whole_seq_spec(s, w), _whole_seq_spec(s, w)],
        out_specs=pl.BlockSpec((1, tq, w), lambda bi, i: (bi, i, 0)),
        out_shape=jax.ShapeDtypeStruct((b, s, w), BF16),
        scratch_shapes=[pltpu.VMEM((SB_HEADS, tq, 1), F32), pltpu.VMEM((SB_HEADS, tq, LANES), F32),
                        pltpu.VMEM((SB_HEADS, tq, LANES), BF16),
                        pltpu.VMEM((SB_HEADS, tq, tq), F32),
                        pltpu.VMEM((SB_HEADS, tq, tq), BF16), pltpu.VMEM((SB_HEADS, tq, tq), BF16),
                        pltpu.VMEM((SB_HEADS, tq, tq), BF16)],
        compiler_params=pltpu.CompilerParams(dimension_semantics=("arbitrary", "arbitrary"),
                                             vmem_limit_bytes=VMEM_LIMIT),
        name="sb_prompt",
    )(qa16, ka16, va16)


def _topk_threshold(count_ge, topk, shape):
    def bit_body(it, u):
        bit = lax.shift_left(jnp.int32(1), 31 - it)
        cand = u | bit
        c = count_ge(cand ^ INT_MIN)
        return jnp.where(c >= topk, cand, u)

    u = lax.fori_loop(0, 32, bit_body, jnp.zeros(shape, jnp.int32))
    return u ^ INT_MIN


def _select(key, tau, need, run, tri_prefix):
    eq = key == tau
    ties = jnp.where(eq, 1.0, 0.0)
    pref = _dot(ties.astype(BF16), tri_prefix) + run
    sel = jnp.logical_or(key > tau, jnp.logical_and(eq, pref <= need))
    return sel, run + jnp.sum(ties, axis=-1, keepdims=True)


def _softmax_step(z, sel, m, l, acc, pv):
    zm = jnp.where(sel, z, NEG_BIG)
    m_new = jnp.maximum(m, jnp.max(zm, axis=-1, keepdims=True))
    alpha = jnp.exp(m - m_new)
    p = jnp.where(sel, jnp.exp(z - m_new), 0.0)
    l_new = alpha * l + jnp.sum(p, axis=-1, keepdims=True)
    acc_new = alpha * acc + pv(p.astype(BF16))
    return m_new, l_new, acc_new


BISECT_UNROLL = 4
ONES_ROWS = 16
TINY_DENOMINATOR = 1e-30


def _pv_width(unroll):
    return 2 if unroll % 2 == 0 else 1


def _dsa_prompt_kernel(qbT_ref, qiT_ref, wiT_ref, kb_ref, vbT_ref, ki_ref, oT_ref,
                       keys_ref, qs_ref, qg_ref, m_ref, mx_ref, acc_ref, p_ref, kn2_ref, *, tq, topk, unroll):
    i = pl.program_id(1)
    tk = KEY_BLOCK
    rep = DSA_HEADS // DSA_KV_HEADS
    kw = _pv_width(unroll)
    q_pos = i * tq + _lane_iota((tk, tq))
    key_off = _row_iota((tk, tq))
    idx_scale = IDX_DIM ** -0.5

    for h in range(IDX_HEADS):
        qs_ref[:, h * tq:(h + 1) * tq] = qiT_ref[0, h * IDX_DIM:(h + 1) * IDX_DIM, :]
    qg_ref[...] = jnp.zeros(qg_ref.shape, BF16)
    for h in range(DSA_HEADS):
        g, r = divmod(h, rep)
        half = g % 2
        qg_ref[g, half * HEAD_DIM:(half + 1) * HEAD_DIM, r * tq:(r + 1) * tq] = (
            qbT_ref[0, h * HEAD_DIM:(h + 1) * HEAD_DIM, :])

    n_trips = (i + unroll) // unroll

    w_scaled = wiT_ref[0] * idx_scale

    def score_body(jg, carry):
        for k in range(unroll):
            j = jg * unroll + k
            start = pl.multiple_of(j * tk, tk)
            sc = _dot(ki_ref[0, pl.ds(start, tk), :], qs_ref[...])
            tot = jnp.zeros((tk, tq), F32)
            for h in range(IDX_HEADS):
                tot = tot + jnp.maximum(sc[:, h * tq:(h + 1) * tq], 0.0) * w_scaled[h:h + 1, :]
            causal = (j * tk + key_off) <= q_pos
            keys_ref[j] = jnp.where(causal, _sort_key(tot), INT_MIN)
        return carry

    lax.fori_loop(0, n_trips, score_body, 0)

    def count(pred):
        def body(jg, c):
            for k in range(unroll):
                c = c + jnp.where(pred(keys_ref[jg * unroll + k]), 1.0, 0.0)
            return c
        c = lax.fori_loop(0, n_trips, body, jnp.zeros((tk, tq), F32))
        return jnp.sum(c, axis=0, keepdims=True)

    tau = _topk_threshold(lambda v: count(lambda key: key >= v), float(topk), (1, tq))
    need = float(topk) - count(lambda key: key > tau)
    surplus_ties = jnp.max(count(lambda key: key >= tau)) > float(topk)

    tri_prefix = _tri(tk, lower=True)

    def selected(j, run, ordered_ties):
        key = keys_ref[j]
        if ordered_ties:
            eq = key == tau
            ties = jnp.where(eq, 1.0, 0.0)
            pref = _dot(tri_prefix, ties.astype(BF16)) + run
            sel = jnp.logical_or(key > tau, jnp.logical_and(eq, pref <= need))
            run = run + jnp.sum(ties, axis=0, keepdims=True)
        else:
            sel = key >= tau
        sel = jnp.logical_and(sel, (j * tk + key_off) <= q_pos)
        return jnp.concatenate([sel] * rep, axis=1), run

    def logits(j, g):
        start = pl.multiple_of(j * tk, tk)
        t = g // 2
        return _dot(kb_ref[0, pl.ds(start, tk), t * LANES:(t + 1) * LANES], qg_ref[g])

    def max_body(jg, run, ordered_ties):
        for k in range(unroll):
            j = jg * unroll + k
            sel_g, run = selected(j, run, ordered_ties)
            for g in range(DSA_KV_HEADS):
                mx_ref[g] = jnp.maximum(mx_ref[g], jnp.where(sel_g, logits(j, g), NEG_BIG))
        return run

    def attn_body(jg, run, ordered_ties):
        for k in range(unroll):
            j = jg * unroll + k
            sel_g, run = selected(j, run, ordered_ties)
            for g in range(DSA_KV_HEADS):
                p = jnp.where(sel_g, jnp.exp(logits(j, g) - m_ref[g]), 0.0)
                p_ref[k // kw, g, (k % kw) * tk:(k % kw + 1) * tk, :] = p.astype(BF16)
        n_wide = unroll // kw
        for g in range(DSA_KV_HEADS):
            pv = _dot(vbT_ref[0, jg * n_wide, g], p_ref[0, g])
            for k in range(1, n_wide):
                pv = pv + _dot(vbT_ref[0, jg * n_wide + k, g], p_ref[k, g])
            acc_ref[g] += pv
        return run

    run0 = jnp.zeros((1, tq), F32)

    def attend(ordered_ties):
        acc_ref[...] = jnp.zeros(acc_ref.shape, F32)
        lax.fori_loop(0, n_trips, functools.partial(attn_body, ordered_ties=ordered_ties), run0)

    @pl.when(i == 0)
    def _():
        lane = _lane_iota((tk, LANES))
        for g in range(DSA_KV_HEADS):
            t, half = divmod(g, 2)
            keep = (lane < HEAD_DIM) if half == 0 else (lane >= HEAD_DIM)

            def norm_body(j, best):
                start = pl.multiple_of(j * tk, tk)
                kf = kb_ref[0, pl.ds(start, tk), t * LANES:(t + 1) * LANES].astype(F32)
                return jnp.maximum(best, jnp.sum(jnp.where(keep, kf * kf, 0.0), axis=1, keepdims=True))

            best = lax.fori_loop(0, kb_ref.shape[1] // tk, norm_body, jnp.zeros((tk, 1), F32))
            kn2_ref[g] = jnp.max(best)

    for g in range(DSA_KV_HEADS):
        qn2 = []
        for r in range(rep):
            h = g * rep + r
            qf = qbT_ref[0, h * HEAD_DIM:(h + 1) * HEAD_DIM, :].astype(F32)
            qn2.append(jnp.sum(qf * qf, axis=0, keepdims=True))
        m_ref[g] = jnp.sqrt(jnp.concatenate(qn2, axis=1) * kn2_ref[g])

    lax.cond(surplus_ties, lambda: attend(True), lambda: attend(False))

    def smallest_denominator():
        lo = jnp.min(acc_ref[0, HEAD_DIM:HEAD_DIM + 1, :])
        for g in range(1, DSA_KV_HEADS):
            lo = jnp.minimum(lo, jnp.min(acc_ref[g, HEAD_DIM:HEAD_DIM + 1, :]))
        return lo

    @pl.when(jnp.logical_not(smallest_denominator() > TINY_DENOMINATOR))
    def _():
        mx_ref[...] = jnp.full(mx_ref.shape, NEG_BIG, F32)
        lax.fori_loop(0, n_trips, functools.partial(max_body, ordered_ties=True), run0)
        for g in range(DSA_KV_HEADS):
            m_ref[g] = jnp.max(mx_ref[g], axis=0, keepdims=True)
        attend(True)

    for h in range(DSA_HEADS):
        g, r = divmod(h, rep)
        o = (acc_ref[g, 0:HEAD_DIM, r * tq:(r + 1) * tq]
             / acc_ref[g, HEAD_DIM:HEAD_DIM + 1, r * tq:(r + 1) * tq])
        oT_ref[0, h * HEAD_DIM:(h + 1) * HEAD_DIM, :] = o.astype(oT_ref.dtype)


def _dsa_prompt(qb16, qi16, wi, kb16, vb16, ki16, *, tq=KEY_BLOCK):
    b, s, _ = qb16.shape
    tk = KEY_BLOCK
    assert s % tq == 0 and tq == tk
    n_kb = s // tk
    topk = min(TOPK_MAX, s // 4)
    unroll = BISECT_UNROLL if n_kb % BISECT_UNROLL == 0 else 1
    rep = DSA_HEADS // DSA_KV_HEADS
    tr = lambda a: jnp.transpose(a, (0, 2, 1))
    kw = _pv_width(unroll)
    vbT = jnp.transpose(vb16.reshape(b, n_kb // kw, kw * tk, DSA_KV_HEADS, HEAD_DIM), (0, 1, 3, 4, 2))
    vbT = jnp.concatenate([vbT, jnp.ones((b, n_kb // kw, DSA_KV_HEADS, ONES_ROWS, kw * tk), BF16)], axis=3)
    v_rows = HEAD_DIM + ONES_ROWS
    qcol = lambda w: pl.BlockSpec((1, w, tq), lambda bi, i: (bi, 0, i))
    whole = lambda shape: pl.BlockSpec((1,) + shape, lambda bi, i: (bi,) + (0,) * len(shape),
                                       pipeline_mode=pl.Buffered(1))
    oT = pl.pallas_call(
        functools.partial(_dsa_prompt_kernel, tq=tq, topk=topk, unroll=unroll),
        grid=(b, s // tq),
        in_specs=[qcol(DSA_Q_W), qcol(IDX_Q_W), qcol(IDX_HEADS),
                  whole((s, DSA_KV_W)), whole((n_kb // kw, DSA_KV_HEADS, v_rows, kw * tk)), whole((s, IDX_DIM))],
        out_specs=qcol(DSA_Q_W),
        out_shape=jax.ShapeDtypeStruct((b, DSA_Q_W, s), BF16),
        scratch_shapes=[
            pltpu.VMEM((n_kb, tk, tq), jnp.int32),
            pltpu.VMEM((IDX_DIM, IDX_HEADS * tq), BF16),
            pltpu.VMEM((DSA_KV_HEADS, LANES, rep * tq), BF16),
            pltpu.VMEM((DSA_KV_HEADS, 1, rep * tq), F32),
            pltpu.VMEM((DSA_KV_HEADS, tk, rep * tq), F32),
            pltpu.VMEM((DSA_KV_HEADS, v_rows, rep * tq), F32),
            pltpu.VMEM((unroll // kw, DSA_KV_HEADS, kw * tk, rep * tq), BF16),
            pltpu.SMEM((DSA_KV_HEADS,), F32),
        ],
        compiler_params=pltpu.CompilerParams(dimension_semantics=("arbitrary", "arbitrary"),
                                             vmem_limit_bytes=VMEM_LIMIT),
        name="dsa_prompt",
    )(tr(qb16), tr(qi16), tr(wi), kb16, vbT, ki16)
    return tr(oT)


def _pages_key_minor(cache):
    c = cache[0]
    n_phys, page = c.shape[:2]
    perm = (0, 2, 3, 1) if c.ndim == 4 else (0, 2, 1)
    return jnp.transpose(c, perm).reshape(n_phys, -1, page)


def _head_rows(q, n_groups, group_of_head):
    bd, t, heads, hd = q.shape
    onehot = np.zeros((heads, n_groups), np.float32)
    for h in range(heads):
        onehot[h, group_of_head(h)] = 1.0
    qt = jnp.transpose(q, (0, 2, 1, 3))
    out = qt[:, :, :, None, :] * jnp.asarray(onehot, q.dtype)[None, :, None, :, None]
    return out.reshape(bd, heads * t, n_groups * hd)


def _sb_sample_kernel(pt_ref, q_ref, kn_ref, vn_ref, kc_hbm, vc_hbm, o_ref,
                      kbuf, vbuf, sem, acc_ref, out_ref, kpad_ref, vpad_ref, *, t, n_pages):
    b = pl.program_id(0)
    rows = SB_HEADS * t
    tri = _tri(KEY_BLOCK, lower=True)

    def page_copies(p, slot):
        page = pt_ref[b, n_pages - 1 - p]
        return (pltpu.make_async_copy(kc_hbm.at[page], kbuf.at[slot], sem.at[0, slot]),
                pltpu.make_async_copy(vc_hbm.at[page], vbuf.at[slot], sem.at[1, slot]))

    def weigh(z, acc, mask):
        l = -_softplus(z)
        if mask is not None:
            l = jnp.where(mask, l, 0.0)
        l_hi, l_lo = _split_bf16(l)
        cs = _dot(l_hi, tri) + _dot(l_lo, tri)
        a = jnp.exp(z + cs + acc)
        if mask is not None:
            a = jnp.where(mask, a, 0.0)
        return a.astype(BF16), acc + cs[:, 0:1]

    def alive(acc):
        return (jnp.max(acc) > SB_DEAD_LOG).astype(jnp.int32)

    for cp in page_copies(0, 0):
        cp.start()

    kpad_ref[...] = jnp.zeros(kpad_ref.shape, BF16)
    vpad_ref[...] = jnp.zeros(vpad_ref.shape, BF16)
    kpad_ref[0:t, :] = kn_ref[0]
    vpad_ref[0:t, :] = vn_ref[0]
    q_time = _row_iota((rows, KEY_BLOCK)) & (t - 1)
    mask = _lane_iota((rows, KEY_BLOCK)) < q_time
    a, acc = weigh(_dot_nt(q_ref[0], kpad_ref[...]), jnp.zeros((rows, 1), F32), mask)
    out_ref[...] = _dot(a, vpad_ref[...])
    acc_ref[...] = acc

    def cond(c):
        p, go = c
        return jnp.logical_and(p < n_pages, go > 0)

    def body(c):
        p, _ = c
        slot = lax.rem(p, 2)
        for cp in page_copies(p, slot):
            cp.wait()

        @pl.when(p + 1 < n_pages)
        def _():
            for cp in page_copies(p + 1, 1 - slot):
                cp.start()

        a, acc = weigh(_dot(q_ref[0], kbuf[slot].astype(BF16)), acc_ref[...], None)
        out_ref[...] += _dot_nt(a, vbuf[slot].astype(BF16))
        acc_ref[...] = acc
        return p + 1, alive(acc)

    p_end, _ = lax.while_loop(cond, body, (jnp.int32(0), alive(acc)))

    @pl.when(p_end < n_pages)
    def _():
        for cp in page_copies(p_end, lax.rem(p_end, 2)):
            cp.wait()

    lane = _lane_iota((t, SB_W))
    res = jnp.zeros((t, SB_W), F32)
    for h in range(SB_HEADS):
        res = res + jnp.where(lane // HEAD_DIM == h, out_ref[h * t:(h + 1) * t, :], 0.0)
    o_ref[0] = res.astype(o_ref.dtype)


def _sb_sample(qa16, ka16, va16, cache_kt, cache_vt, page_table):
    bd, t, w = qa16.shape
    n_pages = page_table.shape[1]
    page = cache_kt.shape[2]
    assert page == KEY_BLOCK and t & (t - 1) == 0 and t % 8 == 0 and n_pages >= 1
    q_rows = _head_rows(qa16.reshape(bd, t, SB_HEADS, HEAD_DIM), SB_HEADS, lambda h: h)
    rows = SB_HEADS * t
    per_b = lambda r, c: pl.BlockSpec((1, r, c), lambda b, pt: (b, 0, 0))
    return pl.pallas_call(
        functools.partial(_sb_sample_kernel, t=t, n_pages=n_pages),
        grid_spec=pltpu.PrefetchScalarGridSpec(
            num_scalar_prefetch=1,
            grid=(bd,),
            in_specs=[per_b(rows, w), per_b(t, w), per_b(t, w),
                      pl.BlockSpec(memory_space=pl.ANY), pl.BlockSpec(memory_space=pl.ANY)],
            out_specs=per_b(t, w),
            scratch_shapes=[pltpu.VMEM((2, w, page), F32), pltpu.VMEM((2, w, page), F32),
                            pltpu.SemaphoreType.DMA((2, 2)),
                            pltpu.VMEM((rows, 1), F32), pltpu.VMEM((rows, w), F32),
                            pltpu.VMEM((KEY_BLOCK, w), BF16), pltpu.VMEM((KEY_BLOCK, w), BF16)],
        ),
        out_shape=jax.ShapeDtypeStruct((bd, t, w), BF16),
        compiler_params=pltpu.CompilerParams(dimension_semantics=("arbitrary",), vmem_limit_bytes=VMEM_LIMIT),
        name="sb_sample",
    )(page_table, q_rows, ka16, va16, cache_kt, cache_vt)


PAGES_PER_STEP = 16
THRESH_GROUP = 16


def _dsa_scores_kernel(pt_ref, q_ref, w_ref, kn_ref, *rest, t, n_pages, pp):
    kc_refs = rest[:pp]
    keys_ref, keysn_ref, kpad_ref = rest[pp:]
    step = pl.program_id(1)
    idx_scale = IDX_DIM ** -0.5

    w_scaled = w_ref[0] * idx_scale

    def keys_of(sc):
        rel = jnp.maximum(sc, 0.0) * w_scaled
        tot = jnp.zeros((t, sc.shape[1]), F32)
        for h in range(IDX_HEADS):
            tot = tot + rel[h * t:(h + 1) * t, :]
        return _sort_key(tot)

    kt = jnp.concatenate([r[0].astype(BF16) for r in kc_refs], axis=1)
    key = keys_of(_dot(q_ref[0], kt))
    for k in range(pp):
        keys_ref[0, step * pp + k] = key[:, k * KEY_BLOCK:(k + 1) * KEY_BLOCK]

    @pl.when(step == n_pages // pp - 1)
    def _():
        kpad_ref[...] = jnp.zeros(kpad_ref.shape, BF16)
        kpad_ref[0:t, :] = kn_ref[0]
        causal = _lane_iota((t, KEY_BLOCK)) <= _row_iota((t, KEY_BLOCK))
        keysn_ref[0] = jnp.where(causal, keys_of(_dot_nt(q_ref[0], kpad_ref[...])), INT_MIN)


def _dsa_thresh_kernel(keys_ref, keysn_ref, tau_ref, need_ref, *, topk):
    def count(cmp, v):
        c = (jnp.sum(jnp.where(cmp(keys_ref[...], v[:, None]), 1.0, 0.0), axis=1)
             + jnp.where(cmp(keysn_ref[...], v), 1.0, 0.0))
        return jnp.sum(c, axis=-1, keepdims=True)

    tau = _topk_threshold(lambda v: count(jnp.greater_equal, v), float(topk), tau_ref.shape)
    tau_ref[...] = tau
    need_ref[...] = float(topk) - count(jnp.greater, tau)


def _dsa_attn_kernel(pt_ref, q_ref, kn_ref, vn_ref, keys_ref, keysn_ref, tau_ref, need_ref, *rest, t, n_pages, pp):
    kc_refs, vc_refs = rest[:pp], rest[pp:2 * pp]
    o_ref, m_ref, l_ref, acc_ref, run_ref, kpad_ref, vpad_ref = rest[2 * pp:]
    step = pl.program_id(1)
    tri_prefix = _tri(KEY_BLOCK, lower=False)
    tau, need = tau_ref[0], need_ref[0]

    @pl.when(step == 0)
    def _():
        m_ref[...] = jnp.full(m_ref.shape, NEG_BIG, F32)
        l_ref[...] = jnp.zeros(l_ref.shape, F32)
        acc_ref[...] = jnp.zeros(acc_ref.shape, F32)
        run_ref[...] = jnp.zeros(run_ref.shape, F32)

    def update(z, sel, pv):
        sel_rows = jnp.concatenate([sel] * DSA_HEADS, axis=0)
        m_ref[...], l_ref[...], acc_ref[...] = _softmax_step(z, sel_rows, m_ref[...], l_ref[...], acc_ref[...], pv)

    sels = []
    run = run_ref[...]
    for k in range(pp):
        sel, run = _select(keys_ref[0, k], tau, need, run, tri_prefix)
        sels.append(sel)
    kt = jnp.concatenate([r[0].astype(BF16) for r in kc_refs], axis=1)

    def pv_pages(p16):
        tot = _dot_nt(p16[:, 0:KEY_BLOCK], vc_refs[0][0].astype(BF16))
        for k in range(1, pp):
            tot = tot + _dot_nt(p16[:, k * KEY_BLOCK:(k + 1) * KEY_BLOCK], vc_refs[k][0].astype(BF16))
        return tot

    update(_dot(q_ref[0], kt), jnp.concatenate(sels, axis=1), pv_pages)
    run_ref[...] = run

    @pl.when(step == n_pages // pp - 1)
    def _():
        kpad_ref[...] = jnp.zeros(kpad_ref.shape, BF16)
        vpad_ref[...] = jnp.zeros(vpad_ref.shape, BF16)
        kpad_ref[0:t, :] = kn_ref[0]
        vpad_ref[0:t, :] = vn_ref[0]
        sel, _ = _select(keysn_ref[0], tau, need, run, tri_prefix)
        causal = _lane_iota((t, KEY_BLOCK)) <= _row_iota((t, KEY_BLOCK))
        update(_dot_nt(q_ref[0], kpad_ref[...]), jnp.logical_and(sel, causal), lambda p16: _dot(p16, vpad_ref[...]))
        o = acc_ref[...] / l_ref[...]
        lane = _lane_iota((t, DSA_KV_W))
        for h in range(DSA_HEADS):
            g = h // (DSA_HEADS // DSA_KV_HEADS)
            o_ref[0, h] = jnp.where(lane // HEAD_DIM == g, o[h * t:(h + 1) * t, :], 0.0)


def _dsa_sample(qb16, kb16, vb16, qi16, ki16, wi, cache_kt, cache_vt, cache_it, page_table):
    bd, t, _ = qb16.shape
    n_pages = page_table.shape[1]
    page = cache_kt.shape[2]
    pp = min(PAGES_PER_STEP, n_pages)
    assert page == KEY_BLOCK and t % 8 == 0 and n_pages % pp == 0
    n_steps = n_pages // pp
    topk = min(TOPK_MAX, (n_pages * page + t) // 4)
    rep = DSA_HEADS // DSA_KV_HEADS

    qi_rows = jnp.transpose(qi16.reshape(bd, t, IDX_HEADS, IDX_DIM), (0, 2, 1, 3)).reshape(bd, IDX_HEADS * t, IDX_DIM)
    wi_rows = jnp.transpose(wi, (0, 2, 1)).reshape(bd, IDX_HEADS * t, 1)
    qb_rows = _head_rows(qb16.reshape(bd, t, DSA_HEADS, HEAD_DIM), DSA_KV_HEADS, lambda h: h // rep)

    per_b = lambda r, c: pl.BlockSpec((1, r, c), lambda b, s, pt: (b, 0, 0))
    page_spec = lambda w, k: pl.BlockSpec((1, w, page), lambda b, s, pt: (pt[b, s * pp + k], 0, 0))
    cparams = pltpu.CompilerParams(dimension_semantics=("arbitrary", "arbitrary"), vmem_limit_bytes=VMEM_LIMIT)

    keys, keys_new = pl.pallas_call(
        functools.partial(_dsa_scores_kernel, t=t, n_pages=n_pages, pp=pp),
        grid_spec=pltpu.PrefetchScalarGridSpec(
            num_scalar_prefetch=1,
            grid=(bd, n_steps),
            in_specs=[per_b(IDX_HEADS * t, IDX_DIM), per_b(IDX_HEADS * t, 1), per_b(t, IDX_DIM)]
                     + [page_spec(IDX_DIM, k) for k in range(pp)],
            out_specs=[pl.BlockSpec((1, n_pages, t, KEY_BLOCK), lambda b, s, pt: (b, 0, 0, 0)),
                       per_b(t, KEY_BLOCK)],
            scratch_shapes=[pltpu.VMEM((KEY_BLOCK, IDX_DIM), BF16)],
        ),
        out_shape=[jax.ShapeDtypeStruct((bd, n_pages, t, KEY_BLOCK), jnp.int32),
                   jax.ShapeDtypeStruct((bd, t, KEY_BLOCK), jnp.int32)],
        compiler_params=cparams,
        name="dsa_scores",
    )(page_table, qi_rows, wi_rows, ki16, *([cache_it] * pp))

    grp = THRESH_GROUP if bd % THRESH_GROUP == 0 else 1
    tau, need = pl.pallas_call(
        functools.partial(_dsa_thresh_kernel, topk=topk),
        grid=(bd // grp,),
        in_specs=[pl.BlockSpec((grp, n_pages, t, KEY_BLOCK), lambda i: (i, 0, 0, 0)),
                  pl.BlockSpec((grp, t, KEY_BLOCK), lambda i: (i, 0, 0))],
        out_specs=[pl.BlockSpec((grp, t, 1), lambda i: (i, 0, 0))] * 2,
        out_shape=[jax.ShapeDtypeStruct((bd, t, 1), jnp.int32), jax.ShapeDtypeStruct((bd, t, 1), F32)],
        compiler_params=pltpu.CompilerParams(dimension_semantics=("arbitrary",), vmem_limit_bytes=VMEM_LIMIT),
        name="dsa_thresh",
    )(keys, keys_new)

    rows = DSA_HEADS * t
    o_rows = pl.pallas_call(
        functools.partial(_dsa_attn_kernel, t=t, n_pages=n_pages, pp=pp),
        grid_spec=pltpu.PrefetchScalarGridSpec(
            num_scalar_prefetch=1,
            grid=(bd, n_steps),
            in_specs=[per_b(rows, DSA_KV_W), per_b(t, DSA_KV_W), per_b(t, DSA_KV_W),
                      pl.BlockSpec((1, pp, t, KEY_BLOCK), lambda b, s, pt: (b, s, 0, 0)),
                      per_b(t, KEY_BLOCK), per_b(t, 1), per_b(t, 1)]
                     + [page_spec(DSA_KV_W, k) for k in range(pp)] * 2,
            out_specs=pl.BlockSpec((1, DSA_HEADS, t, DSA_KV_W), lambda b, s, pt: (b, 0, 0, 0)),
            scratch_shapes=[pltpu.VMEM((rows, 1), F32), pltpu.VMEM((rows, 1), F32), pltpu.VMEM((rows, DSA_KV_W), F32),
                            pltpu.VMEM((t, 1), F32),
                            pltpu.VMEM((KEY_BLOCK, DSA_KV_W), BF16), pltpu.VMEM((KEY_BLOCK, DSA_KV_W), BF16)],
        ),
        out_shape=jax.ShapeDtypeStruct((bd, DSA_HEADS, t, DSA_KV_W), F32),
        compiler_params=cparams,
        name="dsa_attn",
    )(page_table, qb_rows, kb16, vb16, keys, keys_new, tau, need, *([cache_kt] * pp), *([cache_vt] * pp))
    o = o_rows.reshape(bd, DSA_HEADS, t, DSA_KV_HEADS, HEAD_DIM).sum(axis=3)
    return jnp.transpose(o, (0, 2, 1, 3)).reshape(bd, t, DSA_Q_W)


def _out_kernel(x_ref, oa_ref, ob_ref, ga_ref, gb_ref, wpa_ref, wpb_ref, wo_ref, gf_ref, wg_ref, wu_ref, wd_ref,
                gfin_ref, y_ref):
    m = ga_ref[...] * _dot(oa_ref[...], wpa_ref[...]) + gb_ref[...] * _dot(ob_ref[...], wpb_ref[...])
    x = x_ref[...] + _dot(m.astype(BF16), wo_ref[...])
    h = x * lax.rsqrt(jnp.mean(x * x, axis=-1, keepdims=True) + RMS_EPS) * gf_ref[...]
    h16 = h.astype(BF16)
    gate = _dot(h16, wg_ref[...])
    up = _dot(h16, wu_ref[...])
    f = _dot((gate * jax.nn.sigmoid(gate) * up).astype(BF16), wd_ref[...])
    x = x + f
    y_ref[...] = x * lax.rsqrt(jnp.mean(x * x, axis=-1, keepdims=True) + RMS_EPS) * gfin_ref[...]


def _out(x2d, oa16, ob16, ga, gb, w_pa, w_pb, w_o, g_ffn, w_gate, w_up, w_down, g_final, *, tm):
    n, d = x2d.shape
    tm = min(tm, n)
    assert n % tm == 0
    row = lambda w: pl.BlockSpec((tm, w), lambda i: (i, 0))
    const = lambda a: pl.BlockSpec(a.shape, lambda i: (0, 0), pipeline_mode=pl.Buffered(1))
    weights = [w_pa, w_pb, w_o]
    ffn = [w_gate, w_up, w_down]
    gf, gfin = g_ffn.reshape(1, d), g_final.reshape(1, d)
    args = [x2d, oa16, ob16, ga, gb, *weights, gf, *ffn, gfin]
    in_specs = [row(d), row(oa16.shape[1]), row(ob16.shape[1]), row(d), row(d),
                *[const(w) for w in weights], const(gf), *[const(w) for w in ffn], const(gfin)]
    return pl.pallas_call(
        _out_kernel,
        grid=(n // tm,),
        in_specs=in_specs,
        out_specs=row(d),
        out_shape=jax.ShapeDtypeStruct((n, d), F32),
        compiler_params=pltpu.CompilerParams(dimension_semantics=("arbitrary",), vmem_limit_bytes=VMEM_LIMIT),
        name="out",
    )(*args)


def kernel(x_prompt, x_sample, cache_sb_k, cache_sb_v, cache_dsa_k, cache_dsa_v, cache_idx_k, page_table,
           g_mix, w_in, w_pa, w_pb, w_o, g_ffn, w_gate, w_up, w_down, g_final):
    b, s, d = x_prompt.shape
    bd, t, _ = x_sample.shape
    depth, n_phys, page = cache_sb_k.shape[:3]
    assert depth == 1, "single-layer step"
    past = page_table.shape[1] * page
    n_pg = s // page

    w_packed = _pack_w_in(w_in[0])
    pos_p = jnp.arange(s)
    pos_s = past + jnp.arange(bd * t) % t
    xp2 = x_prompt.reshape(b * s, d)
    xs2 = x_sample.reshape(bd * t, d)

    (ka, va, kb, vb, ki, qa16, ka16, va16, qb16, kb16, vb16, qi16, ki16, wi, ga, gb) = _proj(
        xp2, pos_p, g_mix[0], w_packed, tm=256)
    (ska, sva, skb, svb, ski, sqa16, ska16, sva16, sqb16, skb16, svb16, sqi16, ski16, swi, sga, sgb) = _proj(
        xs2, pos_s, g_mix[0], w_packed, tm=256)

    seq = lambda a: a.reshape(b, s, a.shape[-1])
    dec = lambda a: a.reshape(bd, t, a.shape[-1])

    oa = _sb_prompt(seq(qa16), seq(ka16), seq(va16))
    ob = _dsa_prompt(seq(qb16), seq(qi16), seq(wi), seq(kb16), seq(vb16), seq(ki16))
    b16 = lambda w: w[0].astype(BF16)
    out_w = (b16(w_pa), b16(w_pb), b16(w_o), g_ffn[0], b16(w_gate), b16(w_up), b16(w_down), g_final)
    y_prompt = _out(xp2, oa.reshape(b * s, SB_W), ob.reshape(b * s, DSA_Q_W), ga, gb, *out_w, tm=256).reshape(b, s, d)

    soa = _sb_sample(dec(sqa16), dec(ska16), dec(sva16), _pages_key_minor(cache_sb_k), _pages_key_minor(cache_sb_v),
                     page_table)
    sob = _dsa_sample(dec(sqb16), dec(skb16), dec(svb16), dec(sqi16), dec(ski16), dec(swi),
                      _pages_key_minor(cache_dsa_k), _pages_key_minor(cache_dsa_v), _pages_key_minor(cache_idx_k),
                      page_table)
    y_sample = _out(xs2, soa.reshape(bd * t, SB_W), sob.reshape(bd * t, DSA_Q_W).astype(BF16), sga, sgb, *out_w,
                    tm=256).reshape(bd, t, d)

    return (y_prompt, y_sample,
            ka.reshape(1, b, n_pg, page, SB_HEADS, HEAD_DIM), va.reshape(1, b, n_pg, page, SB_HEADS, HEAD_DIM),
            kb.reshape(1, b, n_pg, page, DSA_KV_HEADS, HEAD_DIM), vb.reshape(1, b, n_pg, page, DSA_KV_HEADS, HEAD_DIM),
            ki.reshape(1, b, n_pg, page, IDX_DIM),
            ska.reshape(1, bd, t, SB_HEADS, HEAD_DIM), sva.reshape(1, bd, t, SB_HEADS, HEAD_DIM),
            skb.reshape(1, bd, t, DSA_KV_HEADS, HEAD_DIM), svb.reshape(1, bd, t, DSA_KV_HEADS, HEAD_DIM),
            ski.reshape(1, bd, t, IDX_DIM))
```

```python
import functools

import jax
import jax.numpy as jnp
import numpy as np
from jax import lax
from jax.experimental import pallas as pl
from jax.experimental.pallas import tpu as pltpu

HEAD_DIM = 64
SB_HEADS = 8
DSA_HEADS = 8
DSA_KV_HEADS = 4
IDX_HEADS = 8
IDX_DIM = 32
TOPK_MAX = 256
ROPE_THETA = 500000.0
ROPE_FRACTION = 4
RMS_EPS = 1e-6

SB_W = SB_HEADS * HEAD_DIM
DSA_Q_W = DSA_HEADS * HEAD_DIM
DSA_KV_W = DSA_KV_HEADS * HEAD_DIM
IDX_Q_W = IDX_HEADS * IDX_DIM

LANES = 128
KEY_BLOCK = 128
INT_MIN = -(2 ** 31)
SB_DEAD_LOG = -104.0
NEG_BIG = -1e30
VMEM_LIMIT = 56 * 1024 * 1024

F32 = jnp.float32
BF16 = jnp.bfloat16

_NT = (((1,), (1,)), ((), ()))


def _dot(a, b):
    return jnp.dot(a, b, preferred_element_type=F32)


def _dot_nt(a, b):
    return lax.dot_general(a, b, _NT, preferred_element_type=F32)


def _lane_iota(shape):
    return lax.broadcasted_iota(jnp.int32, shape, len(shape) - 1)


def _row_iota(shape):
    return lax.broadcasted_iota(jnp.int32, shape, len(shape) - 2)


def _softplus(z):
    return jnp.maximum(z, 0.0) + jnp.log1p(jnp.exp(-jnp.abs(z)))


def _split_bf16(x):
    hi = x.astype(BF16)
    lo = (x - hi.astype(F32)).astype(BF16)
    return hi, lo


def _tri(n, lower):
    r = lax.broadcasted_iota(jnp.int32, (n, n), 0)
    c = lax.broadcasted_iota(jnp.int32, (n, n), 1)
    m = (r >= c) if lower else (r <= c)
    return jnp.where(m, 1.0, 0.0).astype(BF16)


def _sort_key(score):
    score = jnp.where(score == 0.0, 0.0, score)
    bits = pltpu.bitcast(score, jnp.int32)
    return bits ^ ((bits >> 31) & 0x7FFFFFFF)


_C_QA = 0
_C_KA = _C_QA + SB_W
_C_VA = _C_KA + SB_W
_C_QB = _C_VA + SB_W
_C_KB = _C_QB + DSA_Q_W
_C_VB = _C_KB + DSA_KV_W
_C_QI = _C_VB + DSA_KV_W
_C_KW = _C_QI + IDX_Q_W
_C_GA = _C_KW + LANES


def _rope(x, c, s, head_w):
    half = head_w // ROPE_FRACTION // 2
    first = (_lane_iota(c.shape) & (head_w - 1)) < half
    tiles = []
    for k in range(x.shape[-1] // LANES):
        xt = x[:, k * LANES:(k + 1) * LANES]
        rot = jnp.where(first, pltpu.roll(xt, LANES - half, 1), pltpu.roll(xt, half, 1))
        tiles.append(xt * c + rot * s)
    return tiles[0] if len(tiles) == 1 else jnp.concatenate(tiles, axis=1)


def _proj_kernel(x_ref, g_ref, w_ref, c64_ref, s64_ref, c32_ref, s32_ref,
                 ka_ref, va_ref, kb_ref, vb_ref, ki_ref,
                 qa16_ref, ka16_ref, va16_ref, qb16_ref, kb16_ref, vb16_ref, qi16_ref, ki16_ref,
                 wi_ref, ga_ref, gb_ref, *, d_model):
    x = x_ref[...]
    y = x * lax.rsqrt(jnp.mean(x * x, axis=-1, keepdims=True) + RMS_EPS)
    h = (y * g_ref[...]).astype(BF16)
    scale = HEAD_DIM ** -0.5

    def cols(lo, width):
        return _dot(h, w_ref[:, lo:lo + width])

    qa = cols(_C_QA, SB_W)
    qa16_ref[...] = (qa * scale).astype(BF16)
    ka = cols(_C_KA, SB_W)
    ka_ref[...] = ka
    ka16_ref[...] = ka.astype(BF16)
    va = cols(_C_VA, SB_W)
    va_ref[...] = va
    va16_ref[...] = va.astype(BF16)

    c64, s64 = c64_ref[...], s64_ref[...]
    qb = _rope(cols(_C_QB, DSA_Q_W), c64, s64, HEAD_DIM)
    qb16_ref[...] = (qb * scale).astype(BF16)
    kb = _rope(cols(_C_KB, DSA_KV_W), c64, s64, HEAD_DIM)
    kb_ref[...] = kb
    kb16_ref[...] = kb.astype(BF16)
    vb = cols(_C_VB, DSA_KV_W)
    vb_ref[...] = vb
    vb16_ref[...] = vb.astype(BF16)

    c32, s32 = c32_ref[...], s32_ref[...]
    qi = _rope(cols(_C_QI, IDX_Q_W), c32, s32, IDX_DIM)
    qi16_ref[...] = qi.astype(BF16)
    kw = cols(_C_KW, LANES)
    ki = _rope(kw, c32, s32, IDX_DIM)[:, :IDX_DIM]
    ki_ref[...] = ki
    ki16_ref[...] = ki.astype(BF16)
    wi_ref[...] = kw[:, IDX_DIM:IDX_DIM + IDX_HEADS] * (IDX_HEADS ** -0.5)

    ga_ref[...] = jax.nn.sigmoid(cols(_C_GA, d_model))
    gb_ref[...] = jax.nn.sigmoid(cols(_C_GA + d_model, d_model))


def _rope_tables(pos, head_w):
    half = head_w // ROPE_FRACTION // 2
    inv_freq = ROPE_THETA ** (-jnp.arange(half, dtype=F32) / half)
    ang = pos.astype(F32)[:, None] * inv_freq[None, :]
    cos, sin = jnp.cos(ang), jnp.sin(ang)
    n = pos.shape[0]
    rest = head_w - 2 * half
    c = jnp.concatenate([cos, cos, jnp.ones((n, rest), F32)], axis=1)
    s = jnp.concatenate([-sin, sin, jnp.zeros((n, rest), F32)], axis=1)
    reps = LANES // head_w
    return jnp.tile(c, (1, reps)), jnp.tile(s, (1, reps))


def _pack_w_in(w_in):
    d = w_in.shape[0]
    pad = jnp.zeros((d, LANES - IDX_DIM - IDX_HEADS), w_in.dtype)
    kw_end = _C_KW + IDX_DIM + IDX_HEADS
    return jnp.concatenate([w_in[:, :_C_KW], w_in[:, _C_KW:kw_end], pad, w_in[:, kw_end:]], axis=1).astype(BF16)


def _proj(x2d, pos, g_mix, w_packed, *, tm):
    n, d = x2d.shape
    p = pos.shape[0]
    tm = min(tm, p)
    assert n % tm == 0 and p % tm == 0
    c64, s64 = _rope_tables(pos, HEAD_DIM)
    c32, s32 = _rope_tables(pos, IDX_DIM)
    pblocks = p // tm
    row = lambda w: pl.BlockSpec((tm, w), lambda i: (i, 0))
    tab = pl.BlockSpec((tm, LANES), lambda i: (i % pblocks, 0))
    const = lambda shape: pl.BlockSpec(shape, lambda i: (0, 0))
    widths_f32 = [SB_W, SB_W, DSA_KV_W, DSA_KV_W, IDX_DIM]
    widths_b16 = [SB_W, SB_W, SB_W, DSA_Q_W, DSA_KV_W, DSA_KV_W, IDX_Q_W, IDX_DIM]
    out_shape = ([jax.ShapeDtypeStruct((n, w), F32) for w in widths_f32]
                 + [jax.ShapeDtypeStruct((n, w), BF16) for w in widths_b16]
                 + [jax.ShapeDtypeStruct((n, IDX_HEADS), F32),
                    jax.ShapeDtypeStruct((n, d), F32), jax.ShapeDtypeStruct((n, d), F32)])
    out_specs = [row(w) for w in widths_f32 + widths_b16] + [row(IDX_HEADS), row(d), row(d)]
    return pl.pallas_call(
        functools.partial(_proj_kernel, d_model=d),
        grid=(n // tm,),
        in_specs=[row(d), const((1, d)), const(w_packed.shape), tab, tab, tab, tab],
        out_specs=out_specs,
        out_shape=out_shape,
        compiler_params=pltpu.CompilerParams(dimension_semantics=("arbitrary",), vmem_limit_bytes=VMEM_LIMIT),
        name="proj",
    )(x2d, g_mix.reshape(1, d), w_packed, c64, s64, c32, s32)


def _sb_prompt_kernel(q_ref, k_ref, v_ref, o_ref, acc_ref, out_ref, qm_ref, z_ref, lhi_ref, llo_ref, a_ref, *, tq):
    i = pl.program_id(1)
    n_pairs = SB_HEADS // 2
    tri = _tri(tq, lower=True)
    lane = _lane_iota((tq, LANES))
    diag_mask = _lane_iota((tq, tq)) < _row_iota((tq, tq))

    for h in range(SB_HEADS):
        p, r = divmod(h, 2)
        qt = q_ref[0, :, p * LANES:(p + 1) * LANES]
        keep = (lane < HEAD_DIM) if r == 0 else (lane >= HEAD_DIM)
        qm_ref[h] = jnp.where(keep, qt, jnp.zeros_like(qt))

    def do_block(j, mask, first):
        start = pl.multiple_of(j * tq, tq)
        tile = lambda ref, h: ref[0, pl.ds(start, tq), (h // 2) * LANES:(h // 2 + 1) * LANES]
        for h in range(SB_HEADS):
            z = _dot_nt(qm_ref[h], tile(k_ref, h))
            l = -_softplus(z)
            if mask is not None:
                l = jnp.where(mask, l, 0.0)
            z_ref[h] = z
            lhi_ref[h], llo_ref[h] = _split_bf16(l)
        worst = None
        for h in range(SB_HEADS):
            cs = _dot(lhi_ref[h], tri) + _dot(llo_ref[h], tri)
            acc = jnp.zeros((tq, 1), F32) if first else acc_ref[h]
            a = jnp.exp(z_ref[h] + cs + acc)
            if mask is not None:
                a = jnp.where(mask, a, 0.0)
            a_ref[h] = a.astype(BF16)
            acc = acc + cs[:, 0:1]
            acc_ref[h] = acc
            worst = acc if worst is None else jnp.maximum(worst, acc)
        for h in range(SB_HEADS):
            contrib = _dot(a_ref[h], tile(v_ref, h))
            if first:
                out_ref[h] = contrib
            else:
                out_ref[h] += contrib
        return (jnp.max(worst) > SB_DEAD_LOG).astype(jnp.int32)

    go0 = do_block(i, diag_mask, True)

    def cond(c):
        j, go = c
        return jnp.logical_and(j >= 0, go > 0)

    def body(c):
        j, _ = c
        return j - 1, do_block(j, None, False)

    lax.while_loop(cond, body, (i - 1, go0))

    for p in range(n_pairs):
        o_ref[0, :, p * LANES:(p + 1) * LANES] = jnp.where(
            lane < HEAD_DIM, out_ref[2 * p], out_ref[2 * p + 1]).astype(o_ref.dtype)


def _whole_seq_spec(s, w):
    return pl.BlockSpec((1, s, w), lambda b, i: (b, 0, 0), pipeline_mode=pl.Buffered(1))


def _sb_prompt(qa16, ka16, va16, *, tq=KEY_BLOCK):
    b, s, w = qa16.shape
    assert s % tq == 0 and w == SB_W
    return pl.pallas_call(
        functools.partial(_sb_prompt_kernel, tq=tq),
        grid=(b, s // tq),
        in_specs=[pl.BlockSpec((1, tq, w), lambda bi, i: (bi, i, 0)), _whole_seq_spec(s, w), _whole_seq_spec(s, w)],
        out_specs=pl.BlockSpec((1, tq, w), lambda bi, i: (bi, i, 0)),
        out_shape=jax.ShapeDtypeStruct((b, s, w), BF16),
        scratch_shapes=[pltpu.VMEM((SB_HEADS, tq, 1), F32), pltpu.VMEM((SB_HEADS, tq, LANES), F32),
                        pltpu.VMEM((SB_HEADS, tq, LANES), BF16),
                        pltpu.VMEM((SB_HEADS, tq, tq), F32),
                        pltpu.VMEM((SB_HEADS, tq, tq), BF16), pltpu.VMEM((SB_HEADS, tq, tq), BF16),
                        pltpu.VMEM((SB_HEADS, tq, tq), BF16)],
        compiler_params=pltpu.CompilerParams(dimension_semantics=("arbitrary", "arbitrary"),
                                             vmem_limit_bytes=VMEM_LIMIT),
        name="sb_prompt",
    )(qa16, ka16, va16)


def _topk_threshold(count_ge, topk, shape):
    def bit_body(it, u):
        bit = lax.shift_left(jnp.int32(1), 31 - it)
        cand = u | bit
        c = count_ge(cand ^ INT_MIN)
        return jnp.where(c >= topk, cand, u)

    u = lax.fori_loop(0, 32, bit_body, jnp.zeros(shape, jnp.int32))
    return u ^ INT_MIN


def _select(key, tau, need, run, tri_prefix):
    eq = key == tau
    ties = jnp.where(eq, 1.0, 0.0)
    pref = _dot(ties.astype(BF16), tri_prefix) + run
    sel = jnp.logical_or(key > tau, jnp.logical_and(eq, pref <= need))
    return sel, run + jnp.sum(ties, axis=-1, keepdims=True)


def _softmax_step(z, sel, m, l, acc, pv):
    zm = jnp.where(sel, z, NEG_BIG)
    m_new = jnp.maximum(m, jnp.max(zm, axis=-1, keepdims=True))
    alpha = jnp.exp(m - m_new)
    p = jnp.where(sel, jnp.exp(z - m_new), 0.0)
    l_new = alpha * l + jnp.sum(p, axis=-1, keepdims=True)
    acc_new = alpha * acc + pv(p.astype(BF16))
    return m_new, l_new, acc_new


BISECT_UNROLL = 4
BISECT_SURE_BITS = 20
BISECT_CHECK_EVERY = 4
NO_COUNT = 1e9
ONES_ROWS = 16
TINY_DENOMINATOR = 1e-30


def _pv_width(unroll):
    return 2 if unroll % 2 == 0 else 1


def _dsa_prompt_kernel(qbT_ref, qiT_ref, wiT_ref, kb_ref, vbT_ref, ki_ref, oT_ref,
                       keys_ref, qs_ref, qg_ref, m_ref, mx_ref, acc_ref, p_ref, kn2_ref, *, tq, topk, unroll):
    i = pl.program_id(1)
    tk = KEY_BLOCK
    rep = DSA_HEADS // DSA_KV_HEADS
    kw = _pv_width(unroll)
    q_pos = i * tq + _lane_iota((tk, tq))
    key_off = _row_iota((tk, tq))
    idx_scale = IDX_DIM ** -0.5

    for h in range(IDX_HEADS):
        qs_ref[:, h * tq:(h + 1) * tq] = qiT_ref[0, h * IDX_DIM:(h + 1) * IDX_DIM, :]
    qg_ref[...] = jnp.zeros(qg_ref.shape, BF16)
    for h in range(DSA_HEADS):
        g, r = divmod(h, rep)
        half = g % 2
        qg_ref[g, half * HEAD_DIM:(half + 1) * HEAD_DIM, r * tq:(r + 1) * tq] = (
            qbT_ref[0, h * HEAD_DIM:(h + 1) * HEAD_DIM, :])

    n_trips = (i + unroll) // unroll

    w_scaled = wiT_ref[0] * idx_scale

    def score_body(jg, carry):
        for k in range(unroll):
            j = jg * unroll + k
            start = pl.multiple_of(j * tk, tk)
            sc = _dot(ki_ref[0, pl.ds(start, tk), :], qs_ref[...])
            tot = jnp.zeros((tk, tq), F32)
            for h in range(IDX_HEADS):
                tot = tot + jnp.maximum(sc[:, h * tq:(h + 1) * tq], 0.0) * w_scaled[h:h + 1, :]
            causal = (j * tk + key_off) <= q_pos
            keys_ref[j] = jnp.where(causal, _sort_key(tot), INT_MIN)
        return carry

    lax.fori_loop(0, n_trips, score_body, 0)

    def count(pred):
        def body(jg, c):
            for k in range(unroll):
                c = c + jnp.where(pred(keys_ref[jg * unroll + k]), 1.0, 0.0)
            return c
        c = lax.fori_loop(0, n_trips, body, jnp.zeros((tk, tq), F32))
        return jnp.sum(c, axis=0, keepdims=True)

    few_keys = (i * tq + _lane_iota((1, tq)) + 1) <= topk

    def bit_step(it, carry):
        u, held = carry
        cand = u | lax.shift_left(jnp.int32(1), 31 - it)
        c = count(lambda key: key >= (cand ^ INT_MIN))
        take = c >= float(topk)
        return jnp.where(take, cand, u), jnp.where(take, c, held)

    def all_settled(carry):
        _, held = carry
        return jnp.min(jnp.where(jnp.logical_or(held == float(topk), few_keys), 1.0, 0.0)) > 0.5

    search = lax.fori_loop(0, BISECT_SURE_BITS, bit_step,
                           (jnp.zeros((1, tq), jnp.int32), jnp.full((1, tq), NO_COUNT, F32)))
    for first in range(BISECT_SURE_BITS, 32, BISECT_CHECK_EVERY):
        last = min(first + BISECT_CHECK_EVERY, 32)
        search = lax.cond(all_settled(search), lambda c: c,
                          functools.partial(lax.fori_loop, first, last, bit_step), search)
    tau = search[0] ^ INT_MIN
    need = float(topk) - count(lambda key: key > tau)
    surplus_ties = jnp.max(jnp.where(few_keys, 0.0, search[1])) > float(topk)

    tri_prefix = _tri(tk, lower=True)

    def selected(j, run, ordered_ties):
        key = keys_ref[j]
        if ordered_ties:
            eq = key == tau
            ties = jnp.where(eq, 1.0, 0.0)
            pref = _dot(tri_prefix, ties.astype(BF16)) + run
            sel = jnp.logical_or(key > tau, jnp.logical_and(eq, pref <= need))
            run = run + jnp.sum(ties, axis=0, keepdims=True)
        else:
            sel = key >= tau
        sel = jnp.logical_and(sel, (j * tk + key_off) <= q_pos)
        return jnp.concatenate([sel] * rep, axis=1), run

    def logits(j, g):
        start = pl.multiple_of(j * tk, tk)
        t = g // 2
        return _dot(kb_ref[0, pl.ds(start, tk), t * LANES:(t + 1) * LANES], qg_ref[g])

    def max_body(jg, run, ordered_ties):
        for k in range(unroll):
            j = jg * unroll + k
            sel_g, run = selected(j, run, ordered_ties)
            for g in range(DSA_KV_HEADS):
                mx_ref[g] = jnp.maximum(mx_ref[g], jnp.where(sel_g, logits(j, g), NEG_BIG))
        return run

    def attn_body(jg, run, ordered_ties):
        for k in range(unroll):
            j = jg * unroll + k
            sel_g, run = selected(j, run, ordered_ties)
            for g in range(DSA_KV_HEADS):
                p = jnp.where(sel_g, jnp.exp(logits(j, g) - m_ref[g]), 0.0)
                p_ref[k // kw, g, (k % kw) * tk:(k % kw + 1) * tk, :] = p.astype(BF16)
        n_wide = unroll // kw
        for g in range(DSA_KV_HEADS):
            pv = _dot(vbT_ref[0, jg * n_wide, g], p_ref[0, g])
            for k in range(1, n_wide):
                pv = pv + _dot(vbT_ref[0, jg * n_wide + k, g], p_ref[k, g])
            acc_ref[g] += pv
        return run

    run0 = jnp.zeros((1, tq), F32)

    def attend(ordered_ties):
        acc_ref[...] = jnp.zeros(acc_ref.shape, F32)
        lax.fori_loop(0, n_trips, functools.partial(attn_body, ordered_ties=ordered_ties), run0)

    @pl.when(i == 0)
    def _():
        lane = _lane_iota((tk, LANES))
        for g in range(DSA_KV_HEADS):
            t, half = divmod(g, 2)
            keep = (lane < HEAD_DIM) if half == 0 else (lane >= HEAD_DIM)

            def norm_body(j, best):
                start = pl.multiple_of(j * tk, tk)
                kf = kb_ref[0, pl.ds(start, tk), t * LANES:(t + 1) * LANES].astype(F32)
                return jnp.maximum(best, jnp.sum(jnp.where(keep, kf * kf, 0.0), axis=1, keepdims=True))

            best = lax.fori_loop(0, kb_ref.shape[1] // tk, norm_body, jnp.zeros((tk, 1), F32))
            kn2_ref[g] = jnp.max(best)

    for g in range(DSA_KV_HEADS):
        qn2 = []
        for r in range(rep):
            h = g * rep + r
            qf = qbT_ref[0, h * HEAD_DIM:(h + 1) * HEAD_DIM, :].astype(F32)
            qn2.append(jnp.sum(qf * qf, axis=0, keepdims=True))
        m_ref[g] = jnp.sqrt(jnp.concatenate(qn2, axis=1) * kn2_ref[g])

    lax.cond(surplus_ties, lambda: attend(True), lambda: attend(False))

    def smallest_denominator():
        lo = jnp.min(acc_ref[0, HEAD_DIM:HEAD_DIM + 1, :])
        for g in range(1, DSA_KV_HEADS):
            lo = jnp.minimum(lo, jnp.min(acc_ref[g, HEAD_DIM:HEAD_DIM + 1, :]))
        return lo

    @pl.when(jnp.logical_not(smallest_denominator() > TINY_DENOMINATOR))
    def _():
        mx_ref[...] = jnp.full(mx_ref.shape, NEG_BIG, F32)
        lax.fori_loop(0, n_trips, functools.partial(max_body, ordered_ties=True), run0)
        for g in range(DSA_KV_HEADS):
            m_ref[g] = jnp.max(mx_ref[g], axis=0, keepdims=True)
        attend(True)

    for h in range(DSA_HEADS):
        g, r = divmod(h, rep)
        o = (acc_ref[g, 0:HEAD_DIM, r * tq:(r + 1) * tq]
             / acc_ref[g, HEAD_DIM:HEAD_DIM + 1, r * tq:(r + 1) * tq])
        oT_ref[0, h * HEAD_DIM:(h + 1) * HEAD_DIM, :] = o.astype(oT_ref.dtype)


def _dsa_prompt(qb16, qi16, wi, kb16, vb16, ki16, *, tq=KEY_BLOCK):
    b, s, _ = qb16.shape
    tk = KEY_BLOCK
    assert s % tq == 0 and tq == tk
    n_kb = s // tk
    topk = min(TOPK_MAX, s // 4)
    unroll = BISECT_UNROLL if n_kb % BISECT_UNROLL == 0 else 1
    rep = DSA_HEADS // DSA_KV_HEADS
    tr = lambda a: jnp.transpose(a, (0, 2, 1))
    kw = _pv_width(unroll)
    vbT = jnp.transpose(vb16.reshape(b, n_kb // kw, kw * tk, DSA_KV_HEADS, HEAD_DIM), (0, 1, 3, 4, 2))
    vbT = jnp.concatenate([vbT, jnp.ones((b, n_kb // kw, DSA_KV_HEADS, ONES_ROWS, kw * tk), BF16)], axis=3)
    v_rows = HEAD_DIM + ONES_ROWS
    qcol = lambda w: pl.BlockSpec((1, w, tq), lambda bi, i: (bi, 0, i))
    whole = lambda shape: pl.BlockSpec((1,) + shape, lambda bi, i: (bi,) + (0,) * len(shape),
                                       pipeline_mode=pl.Buffered(1))
    oT = pl.pallas_call(
        functools.partial(_dsa_prompt_kernel, tq=tq, topk=topk, unroll=unroll),
        grid=(b, s // tq),
        in_specs=[qcol(DSA_Q_W), qcol(IDX_Q_W), qcol(IDX_HEADS),
                  whole((s, DSA_KV_W)), whole((n_kb // kw, DSA_KV_HEADS, v_rows, kw * tk)), whole((s, IDX_DIM))],
        out_specs=qcol(DSA_Q_W),
        out_shape=jax.ShapeDtypeStruct((b, DSA_Q_W, s), BF16),
        scratch_shapes=[
            pltpu.VMEM((n_kb, tk, tq), jnp.int32),
            pltpu.VMEM((IDX_DIM, IDX_HEADS * tq), BF16),
            pltpu.VMEM((DSA_KV_HEADS, LANES, rep * tq), BF16),
            pltpu.VMEM((DSA_KV_HEADS, 1, rep * tq), F32),
            pltpu.VMEM((DSA_KV_HEADS, tk, rep * tq), F32),
            pltpu.VMEM((DSA_KV_HEADS, v_rows, rep * tq), F32),
            pltpu.VMEM((unroll // kw, DSA_KV_HEADS, kw * tk, rep * tq), BF16),
            pltpu.SMEM((DSA_KV_HEADS,), F32),
        ],
        compiler_params=pltpu.CompilerParams(dimension_semantics=("arbitrary", "arbitrary"),
                                             vmem_limit_bytes=VMEM_LIMIT),
        name="dsa_prompt",
    )(tr(qb16), tr(qi16), tr(wi), kb16, vbT, ki16)
    return tr(oT)


def _pages_key_minor(cache):
    c = cache[0]
    n_phys, page = c.shape[:2]
    perm = (0, 2, 3, 1) if c.ndim == 4 else (0, 2, 1)
    return jnp.transpose(c, perm).reshape(n_phys, -1, page)


def _head_rows(q, n_groups, group_of_head):
    bd, t, heads, hd = q.shape
    onehot = np.zeros((heads, n_groups), np.float32)
    for h in range(heads):
        onehot[h, group_of_head(h)] = 1.0
    qt = jnp.transpose(q, (0, 2, 1, 3))
    out = qt[:, :, :, None, :] * jnp.asarray(onehot, q.dtype)[None, :, None, :, None]
    return out.reshape(bd, heads * t, n_groups * hd)


def _sb_sample_kernel(pt_ref, q_ref, kn_ref, vn_ref, kc_hbm, vc_hbm, o_ref,
                      kbuf, vbuf, sem, acc_ref, out_ref, kpad_ref, vpad_ref, *, t, n_pages):
    b = pl.program_id(0)
    rows = SB_HEADS * t
    tri = _tri(KEY_BLOCK, lower=True)

    def page_copies(p, slot):
        page = pt_ref[b, n_pages - 1 - p]
        return (pltpu.make_async_copy(kc_hbm.at[page], kbuf.at[slot], sem.at[0, slot]),
                pltpu.make_async_copy(vc_hbm.at[page], vbuf.at[slot], sem.at[1, slot]))

    def weigh(z, acc, mask):
        l = -_softplus(z)
        if mask is not None:
            l = jnp.where(mask, l, 0.0)
        l_hi, l_lo = _split_bf16(l)
        cs = _dot(l_hi, tri) + _dot(l_lo, tri)
        a = jnp.exp(z + cs + acc)
        if mask is not None:
            a = jnp.where(mask, a, 0.0)
        return a.astype(BF16), acc + cs[:, 0:1]

    def alive(acc):
        return (jnp.max(acc) > SB_DEAD_LOG).astype(jnp.int32)

    for cp in page_copies(0, 0):
        cp.start()

    kpad_ref[...] = jnp.zeros(kpad_ref.shape, BF16)
    vpad_ref[...] = jnp.zeros(vpad_ref.shape, BF16)
    kpad_ref[0:t, :] = kn_ref[0]
    vpad_ref[0:t, :] = vn_ref[0]
    q_time = _row_iota((rows, KEY_BLOCK)) & (t - 1)
    mask = _lane_iota((rows, KEY_BLOCK)) < q_time
    a, acc = weigh(_dot_nt(q_ref[0], kpad_ref[...]), jnp.zeros((rows, 1), F32), mask)
    out_ref[...] = _dot(a, vpad_ref[...])
    acc_ref[...] = acc

    def cond(c):
        p, go = c
        return jnp.logical_and(p < n_pages, go > 0)

    def body(c):
        p, _ = c
        slot = lax.rem(p, 2)
        for cp in page_copies(p, slot):
            cp.wait()

        @pl.when(p + 1 < n_pages)
        def _():
            for cp in page_copies(p + 1, 1 - slot):
                cp.start()

        a, acc = weigh(_dot(q_ref[0], kbuf[slot].astype(BF16)), acc_ref[...], None)
        out_ref[...] += _dot_nt(a, vbuf[slot].astype(BF16))
        acc_ref[...] = acc
        return p + 1, alive(acc)

    p_end, _ = lax.while_loop(cond, body, (jnp.int32(0), alive(acc)))

    @pl.when(p_end < n_pages)
    def _():
        for cp in page_copies(p_end, lax.rem(p_end, 2)):
            cp.wait()

    lane = _lane_iota((t, SB_W))
    res = jnp.zeros((t, SB_W), F32)
    for h in range(SB_HEADS):
        res = res + jnp.where(lane // HEAD_DIM == h, out_ref[h * t:(h + 1) * t, :], 0.0)
    o_ref[0] = res.astype(o_ref.dtype)


def _sb_sample(qa16, ka16, va16, cache_kt, cache_vt, page_table):
    bd, t, w = qa16.shape
    n_pages = page_table.shape[1]
    page = cache_kt.shape[2]
    assert page == KEY_BLOCK and t & (t - 1) == 0 and t % 8 == 0 and n_pages >= 1
    q_rows = _head_rows(qa16.reshape(bd, t, SB_HEADS, HEAD_DIM), SB_HEADS, lambda h: h)
    rows = SB_HEADS * t
    per_b = lambda r, c: pl.BlockSpec((1, r, c), lambda b, pt: (b, 0, 0))
    return pl.pallas_call(
        functools.partial(_sb_sample_kernel, t=t, n_pages=n_pages),
        grid_spec=pltpu.PrefetchScalarGridSpec(
            num_scalar_prefetch=1,
            grid=(bd,),
            in_specs=[per_b(rows, w), per_b(t, w), per_b(t, w),
                      pl.BlockSpec(memory_space=pl.ANY), pl.BlockSpec(memory_space=pl.ANY)],
            out_specs=per_b(t, w),
            scratch_shapes=[pltpu.VMEM((2, w, page), F32), pltpu.VMEM((2, w, page), F32),
                            pltpu.SemaphoreType.DMA((2, 2)),
                            pltpu.VMEM((rows, 1), F32), pltpu.VMEM((rows, w), F32),
                            pltpu.VMEM((KEY_BLOCK, w), BF16), pltpu.VMEM((KEY_BLOCK, w), BF16)],
        ),
        out_shape=jax.ShapeDtypeStruct((bd, t, w), BF16),
        compiler_params=pltpu.CompilerParams(dimension_semantics=("arbitrary",), vmem_limit_bytes=VMEM_LIMIT),
        name="sb_sample",
    )(page_table, q_rows, ka16, va16, cache_kt, cache_vt)


PAGES_PER_STEP = 16
THRESH_GROUP = 16


def _dsa_scores_kernel(pt_ref, q_ref, w_ref, kn_ref, *rest, t, n_pages, pp):
    kc_refs = rest[:pp]
    keys_ref, keysn_ref, kpad_ref = rest[pp:]
    step = pl.program_id(1)
    idx_scale = IDX_DIM ** -0.5

    w_scaled = w_ref[0] * idx_scale

    def keys_of(sc):
        rel = jnp.maximum(sc, 0.0) * w_scaled
        tot = jnp.zeros((t, sc.shape[1]), F32)
        for h in range(IDX_HEADS):
            tot = tot + rel[h * t:(h + 1) * t, :]
        return _sort_key(tot)

    kt = jnp.concatenate([r[0].astype(BF16) for r in kc_refs], axis=1)
    key = keys_of(_dot(q_ref[0], kt))
    for k in range(pp):
        keys_ref[0, step * pp + k] = key[:, k * KEY_BLOCK:(k + 1) * KEY_BLOCK]

    @pl.when(step == n_pages // pp - 1)
    def _():
        kpad_ref[...] = jnp.zeros(kpad_ref.shape, BF16)
        kpad_ref[0:t, :] = kn_ref[0]
        causal = _lane_iota((t, KEY_BLOCK)) <= _row_iota((t, KEY_BLOCK))
        keysn_ref[0] = jnp.where(causal, keys_of(_dot_nt(q_ref[0], kpad_ref[...])), INT_MIN)


def _dsa_thresh_kernel(keys_ref, keysn_ref, tau_ref, need_ref, *, topk):
    def count(cmp, v):
        c = (jnp.sum(jnp.where(cmp(keys_ref[...], v[:, None]), 1.0, 0.0), axis=1)
             + jnp.where(cmp(keysn_ref[...], v), 1.0, 0.0))
        return jnp.sum(c, axis=-1, keepdims=True)

    tau = _topk_threshold(lambda v: count(jnp.greater_equal, v), float(topk), tau_ref.shape)
    tau_ref[...] = tau
    need_ref[...] = float(topk) - count(jnp.greater, tau)


def _dsa_attn_kernel(pt_ref, q_ref, kn_ref, vn_ref, keys_ref, keysn_ref, tau_ref, need_ref, *rest, t, n_pages, pp):
    kc_refs, vc_refs = rest[:pp], rest[pp:2 * pp]
    o_ref, m_ref, l_ref, acc_ref, run_ref, kpad_ref, vpad_ref = rest[2 * pp:]
    step = pl.program_id(1)
    tri_prefix = _tri(KEY_BLOCK, lower=False)
    tau, need = tau_ref[0], need_ref[0]

    @pl.when(step == 0)
    def _():
        m_ref[...] = jnp.full(m_ref.shape, NEG_BIG, F32)
        l_ref[...] = jnp.zeros(l_ref.shape, F32)
        acc_ref[...] = jnp.zeros(acc_ref.shape, F32)
        run_ref[...] = jnp.zeros(run_ref.shape, F32)

    def update(z, sel, pv):
        sel_rows = jnp.concatenate([sel] * DSA_HEADS, axis=0)
        m_ref[...], l_ref[...], acc_ref[...] = _softmax_step(z, sel_rows, m_ref[...], l_ref[...], acc_ref[...], pv)

    key_rows = keys_ref[0].reshape(pp * t, KEY_BLOCK)
    tau_rows = jnp.concatenate([tau] * pp, axis=0)
    ties = jnp.where(key_rows == tau_rows, 1.0, 0.0)
    rank = _dot(ties.astype(BF16), tri_prefix)
    held = jnp.sum(ties, axis=-1, keepdims=True)
    sels = []
    run = run_ref[...]
    for k in range(pp):
        rows = slice(k * t, (k + 1) * t)
        sels.append(jnp.logical_or(key_rows[rows] > tau,
                                   jnp.logical_and(key_rows[rows] == tau, rank[rows] + run <= need)))
        run = run + held[rows]
    kt = jnp.concatenate([r[0].astype(BF16) for r in kc_refs], axis=1)

    def pv_pages(p16):
        tot = _dot_nt(p16[:, 0:KEY_BLOCK], vc_refs[0][0].astype(BF16))
        for k in range(1, pp):
            tot = tot + _dot_nt(p16[:, k * KEY_BLOCK:(k + 1) * KEY_BLOCK], vc_refs[k][0].astype(BF16))
        return tot

    update(_dot(q_ref[0], kt), jnp.concatenate(sels, axis=1), pv_pages)
    run_ref[...] = run

    @pl.when(step == n_pages // pp - 1)
    def _():
        kpad_ref[...] = jnp.zeros(kpad_ref.shape, BF16)
        vpad_ref[...] = jnp.zeros(vpad_ref.shape, BF16)
        kpad_ref[0:t, :] = kn_ref[0]
        vpad_ref[0:t, :] = vn_ref[0]
        sel, _ = _select(keysn_ref[0], tau, need, run, tri_prefix)
        causal = _lane_iota((t, KEY_BLOCK)) <= _row_iota((t, KEY_BLOCK))
        update(_dot_nt(q_ref[0], kpad_ref[...]), jnp.logical_and(sel, causal), lambda p16: _dot(p16, vpad_ref[...]))
        o = acc_ref[...] / l_ref[...]
        lane = _lane_iota((t, DSA_KV_W))
        for h in range(DSA_HEADS):
            g = h // (DSA_HEADS // DSA_KV_HEADS)
            o_ref[0, h] = jnp.where(lane // HEAD_DIM == g, o[h * t:(h + 1) * t, :], 0.0)


def _dsa_sample(qb16, kb16, vb16, qi16, ki16, wi, cache_kt, cache_vt, cache_it, page_table):
    bd, t, _ = qb16.shape
    n_pages = page_table.shape[1]
    page = cache_kt.shape[2]
    pp = min(PAGES_PER_STEP, n_pages)
    assert page == KEY_BLOCK and t % 8 == 0 and n_pages % pp == 0
    n_steps = n_pages // pp
    topk = min(TOPK_MAX, (n_pages * page + t) // 4)
    rep = DSA_HEADS // DSA_KV_HEADS

    qi_rows = jnp.transpose(qi16.reshape(bd, t, IDX_HEADS, IDX_DIM), (0, 2, 1, 3)).reshape(bd, IDX_HEADS * t, IDX_DIM)
    wi_rows = jnp.transpose(wi, (0, 2, 1)).reshape(bd, IDX_HEADS * t, 1)
    qb_rows = _head_rows(qb16.reshape(bd, t, DSA_HEADS, HEAD_DIM), DSA_KV_HEADS, lambda h: h // rep)

    per_b = lambda r, c: pl.BlockSpec((1, r, c), lambda b, s, pt: (b, 0, 0))
    page_spec = lambda w, k: pl.BlockSpec((1, w, page), lambda b, s, pt: (pt[b, s * pp + k], 0, 0))
    cparams = pltpu.CompilerParams(dimension_semantics=("arbitrary", "arbitrary"), vmem_limit_bytes=VMEM_LIMIT)

    keys, keys_new = pl.pallas_call(
        functools.partial(_dsa_scores_kernel, t=t, n_pages=n_pages, pp=pp),
        grid_spec=pltpu.PrefetchScalarGridSpec(
            num_scalar_prefetch=1,
            grid=(bd, n_steps),
            in_specs=[per_b(IDX_HEADS * t, IDX_DIM), per_b(IDX_HEADS * t, 1), per_b(t, IDX_DIM)]
                     + [page_spec(IDX_DIM, k) for k in range(pp)],
            out_specs=[pl.BlockSpec((1, n_pages, t, KEY_BLOCK), lambda b, s, pt: (b, 0, 0, 0)),
                       per_b(t, KEY_BLOCK)],
            scratch_shapes=[pltpu.VMEM((KEY_BLOCK, IDX_DIM), BF16)],
        ),
        out_shape=[jax.ShapeDtypeStruct((bd, n_pages, t, KEY_BLOCK), jnp.int32),
                   jax.ShapeDtypeStruct((bd, t, KEY_BLOCK), jnp.int32)],
        compiler_params=cparams,
        name="dsa_scores",
    )(page_table, qi_rows, wi_rows, ki16, *([cache_it] * pp))

    grp = THRESH_GROUP if bd % THRESH_GROUP == 0 else 1
    tau, need = pl.pallas_call(
        functools.partial(_dsa_thresh_kernel, topk=topk),
        grid=(bd // grp,),
        in_specs=[pl.BlockSpec((grp, n_pages, t, KEY_BLOCK), lambda i: (i, 0, 0, 0)),
                  pl.BlockSpec((grp, t, KEY_BLOCK), lambda i: (i, 0, 0))],
        out_specs=[pl.BlockSpec((grp, t, 1), lambda i: (i, 0, 0))] * 2,
        out_shape=[jax.ShapeDtypeStruct((bd, t, 1), jnp.int32), jax.ShapeDtypeStruct((bd, t, 1), F32)],
        compiler_params=pltpu.CompilerParams(dimension_semantics=("arbitrary",), vmem_limit_bytes=VMEM_LIMIT),
        name="dsa_thresh",
    )(keys, keys_new)

    rows = DSA_HEADS * t
    o_rows = pl.pallas_call(
        functools.partial(_dsa_attn_kernel, t=t, n_pages=n_pages, pp=pp),
        grid_spec=pltpu.PrefetchScalarGridSpec(
            num_scalar_prefetch=1,
            grid=(bd, n_steps),
            in_specs=[per_b(rows, DSA_KV_W), per_b(t, DSA_KV_W), per_b(t, DSA_KV_W),
                      pl.BlockSpec((1, pp, t, KEY_BLOCK), lambda b, s, pt: (b, s, 0, 0)),
                      per_b(t, KEY_BLOCK), per_b(t, 1), per_b(t, 1)]
                     + [page_spec(DSA_KV_W, k) for k in range(pp)] * 2,
            out_specs=pl.BlockSpec((1, DSA_HEADS, t, DSA_KV_W), lambda b, s, pt: (b, 0, 0, 0)),
            scratch_shapes=[pltpu.VMEM((rows, 1), F32), pltpu.VMEM((rows, 1), F32), pltpu.VMEM((rows, DSA_KV_W), F32),
                            pltpu.VMEM((t, 1), F32),
                            pltpu.VMEM((KEY_BLOCK, DSA_KV_W), BF16), pltpu.VMEM((KEY_BLOCK, DSA_KV_W), BF16)],
        ),
        out_shape=jax.ShapeDtypeStruct((bd, DSA_HEADS, t, DSA_KV_W), F32),
        compiler_params=cparams,
        name="dsa_attn",
    )(page_table, qb_rows, kb16, vb16, keys, keys_new, tau, need, *([cache_kt] * pp), *([cache_vt] * pp))
    o = o_rows.reshape(bd, DSA_HEADS, t, DSA_KV_HEADS, HEAD_DIM).sum(axis=3)
    return jnp.transpose(o, (0, 2, 1, 3)).reshape(bd, t, DSA_Q_W)


def _out_kernel(x_ref, oa_ref, ob_ref, ga_ref, gb_ref, wpa_ref, wpb_ref, wo_ref, gf_ref, wg_ref, wu_ref, wd_ref,
                gfin_ref, y_ref):
    m = ga_ref[...] * _dot(oa_ref[...], wpa_ref[...]) + gb_ref[...] * _dot(ob_ref[...], wpb_ref[...])
    x = x_ref[...] + _dot(m.astype(BF16), wo_ref[...])
    h = x * lax.rsqrt(jnp.mean(x * x, axis=-1, keepdims=True) + RMS_EPS) * gf_ref[...]
    h16 = h.astype(BF16)
    gate = _dot(h16, wg_ref[...])
    up = _dot(h16, wu_ref[...])
    f = _dot((gate * jax.nn.sigmoid(gate) * up).astype(BF16), wd_ref[...])
    x = x + f
    y_ref[...] = x * lax.rsqrt(jnp.mean(x * x, axis=-1, keepdims=True) + RMS_EPS) * gfin_ref[...]


def _out(x2d, oa16, ob16, ga, gb, w_pa, w_pb, w_o, g_ffn, w_gate, w_up, w_down, g_final, *, tm):
    n, d = x2d.shape
    tm = min(tm, n)
    assert n % tm == 0
    row = lambda w: pl.BlockSpec((tm, w), lambda i: (i, 0))
    const = lambda a: pl.BlockSpec(a.shape, lambda i: (0, 0), pipeline_mode=pl.Buffered(1))
    weights = [w_pa, w_pb, w_o]
    ffn = [w_gate, w_up, w_down]
    gf, gfin = g_ffn.reshape(1, d), g_final.reshape(1, d)
    args = [x2d, oa16, ob16, ga, gb, *weights, gf, *ffn, gfin]
    in_specs = [row(d), row(oa16.shape[1]), row(ob16.shape[1]), row(d), row(d),
                *[const(w) for w in weights], const(gf), *[const(w) for w in ffn], const(gfin)]
    return pl.pallas_call(
        _out_kernel,
        grid=(n // tm,),
        in_specs=in_specs,
        out_specs=row(d),
        out_shape=jax.ShapeDtypeStruct((n, d), F32),
        compiler_params=pltpu.CompilerParams(dimension_semantics=("arbitrary",), vmem_limit_bytes=VMEM_LIMIT),
        name="out",
    )(*args)


def kernel(x_prompt, x_sample, cache_sb_k, cache_sb_v, cache_dsa_k, cache_dsa_v, cache_idx_k, page_table,
           g_mix, w_in, w_pa, w_pb, w_o, g_ffn, w_gate, w_up, w_down, g_final):
    b, s, d = x_prompt.shape
    bd, t, _ = x_sample.shape
    depth, n_phys, page = cache_sb_k.shape[:3]
    assert depth == 1, "single-layer step"
    past = page_table.shape[1] * page
    n_pg = s // page

    w_packed = _pack_w_in(w_in[0])
    pos_p = jnp.arange(s)
    pos_s = past + jnp.arange(bd * t) % t
    xp2 = x_prompt.reshape(b * s, d)
    xs2 = x_sample.reshape(bd * t, d)

    (ka, va, kb, vb, ki, qa16, ka16, va16, qb16, kb16, vb16, qi16, ki16, wi, ga, gb) = _proj(
        xp2, pos_p, g_mix[0], w_packed, tm=256)
    (ska, sva, skb, svb, ski, sqa16, ska16, sva16, sqb16, skb16, svb16, sqi16, ski16, swi, sga, sgb) = _proj(
        xs2, pos_s, g_mix[0], w_packed, tm=256)

    seq = lambda a: a.reshape(b, s, a.shape[-1])
    dec = lambda a: a.reshape(bd, t, a.shape[-1])

    oa = _sb_prompt(seq(qa16), seq(ka16), seq(va16))
    ob = _dsa_prompt(seq(qb16), seq(qi16), seq(wi), seq(kb16), seq(vb16), seq(ki16))
    b16 = lambda w: w[0].astype(BF16)
    out_w = (b16(w_pa), b16(w_pb), b16(w_o), g_ffn[0], b16(w_gate), b16(w_up), b16(w_down), g_final)
    y_prompt = _out(xp2, oa.reshape(b * s, SB_W), ob.reshape(b * s, DSA_Q_W), ga, gb, *out_w, tm=256).reshape(b, s, d)

    soa = _sb_sample(dec(sqa16), dec(ska16), dec(sva16), _pages_key_minor(cache_sb_k), _pages_key_minor(cache_sb_v),
                     page_table)
    sob = _dsa_sample(dec(sqb16), dec(skb16), dec(svb16), dec(sqi16), dec(ski16), dec(swi),
                      _pages_key_minor(cache_dsa_k), _pages_key_minor(cache_dsa_v), _pages_key_minor(cache_idx_k),
                      page_table)
    y_sample = _out(xs2, soa.reshape(bd * t, SB_W), sob.reshape(bd * t, DSA_Q_W).astype(BF16), sga, sgb, *out_w,
                    tm=256).reshape(bd, t, d)

    return (y_prompt, y_sample,
            ka.reshape(1, b, n_pg, page, SB_HEADS, HEAD_DIM), va.reshape(1, b, n_pg, page, SB_HEADS, HEAD_DIM),
            kb.reshape(1, b, n_pg, page, DSA_KV_HEADS, HEAD_DIM), vb.reshape(1, b, n_pg, page, DSA_KV_HEADS, HEAD_DIM),
            ki.reshape(1, b, n_pg, page, IDX_DIM),
            ska.reshape(1, bd, t, SB_HEADS, HEAD_DIM), sva.reshape(1, bd, t, SB_HEADS, HEAD_DIM),
            skb.reshape(1, bd, t, DSA_KV_HEADS, HEAD_DIM), svb.reshape(1, bd, t, DSA_KV_HEADS, HEAD_DIM),
            ski.reshape(1, bd, t, IDX_DIM))
```
